```python
import math
import jax, jax.numpy as jnp
from jax import lax
import numpy as np

D_MODEL = 1024
BATCH = 8
SEQ = 4096
DEPTH = 2

HEAD_DIM = 64
GROUP_WIDTH = D_MODEL // 4
MIX_WIDTH = 4 * GROUP_WIDTH
N_GROUP_HEADS = GROUP_WIDTH // HEAD_DIM
DIFF_HEADS = N_GROUP_HEADS
DIFF_QK_DIM = HEAD_DIM // 2
MLA_HEADS = N_GROUP_HEADS
MLA_NOPE_DIM = HEAD_DIM
MLA_ROPE_DIM = HEAD_DIM // 2
MLA_V_DIM = HEAD_DIM
MLA_Q_RANK = MLA_HEADS * (MLA_NOPE_DIM + MLA_ROPE_DIM)
MLA_KV_RANK = 4 * MLA_V_DIM
GQA_HEADS = N_GROUP_HEADS
GQA_KV_HEADS = GQA_HEADS // 2
GRID_W = 64
DIL_HEADS = N_GROUP_HEADS
DIL_PAIRS = ((128, 1), (512, 4), (2048, 16))
Q_BLOCK = 128
MLP_HIDDEN = 4 * D_MODEL
ROPE_THETA = 10000.0
RMS_EPS = 1e-6
LN_EPS = 1e-5
NEG_INF = -1e30

kernel_name = "hybrid_parallel_head_group_encoder"


def _column_widths():
    a = DIFF_HEADS * 2 * DIFF_QK_DIM
    return [a, a, a,
            MLA_Q_RANK, MLA_KV_RANK, MLA_ROPE_DIM,
            GQA_HEADS * HEAD_DIM, GQA_KV_HEADS * HEAD_DIM, GQA_KV_HEADS * HEAD_DIM,
            DIL_HEADS * HEAD_DIM, DIL_HEADS * HEAD_DIM, DIL_HEADS * HEAD_DIM]


def _split_points():
    pts, acc = [], 0
    for w in _column_widths()[:-1]:
        acc += w
        pts.append(acc)
    return pts


def _alibi_slopes(n):
    return jnp.exp2(-8.0 * (jnp.arange(n, dtype=jnp.float32) + 1.0) / n)


def _rms(x, g):
    xf = x.astype(jnp.float32)
    y = xf * lax.rsqrt(jnp.mean(jnp.square(xf), -1, keepdims=True) + RMS_EPS)
    return y.astype(x.dtype) * g


def _layer_norm(x, g, b):
    xf = x.astype(jnp.float32)
    mu = jnp.mean(xf, -1, keepdims=True)
    var = jnp.mean(jnp.square(xf - mu), -1, keepdims=True)
    return ((xf - mu) * lax.rsqrt(var + LN_EPS)).astype(x.dtype) * g + b


def _rope(x, pos):
    half = x.shape[-1] // 2
    freqs = ROPE_THETA ** (-jnp.arange(half, dtype=jnp.float32) / half)
    ang = pos[:, None] * freqs[None, :]
    cos = jnp.cos(ang)[None, :, None, :]
    sin = jnp.sin(ang)[None, :, None, :]
    xf = x.astype(jnp.float32)
    x1, x2 = xf[..., :half], xf[..., half:]
    return jnp.concatenate([x1 * cos - x2 * sin, x2 * cos + x1 * sin], -1).astype(x.dtype)


def _axial_rope(x, row, col):
    half = x.shape[-1] // 2
    return jnp.concatenate([_rope(x[..., :half], row), _rope(x[..., half:], col)], -1)


def _sweep_query_blocks(fn, *q_arrays):
    bsz, seq = q_arrays[0].shape[:2]
    nb = seq // Q_BLOCK

    def to_blocks(a):
        return jnp.swapaxes(a.reshape(bsz, nb, Q_BLOCK, *a.shape[2:]), 0, 1)

    qpos = jnp.arange(seq, dtype=jnp.int32).reshape(nb, Q_BLOCK)
    out = lax.map(lambda args: fn(*args), (qpos,) + tuple(to_blocks(a) for a in q_arrays))
    out = jnp.swapaxes(out, 0, 1)
    return out.reshape(bsz, seq, *out.shape[3:])


def _diff_attention(q, k, v, lam, subln_g, lambda_init, slopes):
    bsz, seq = q.shape[:2]
    kpos = jnp.arange(seq, dtype=jnp.int32)
    scale = DIFF_QK_DIM ** -0.5

    def block(qpos, qb):
        s = jnp.einsum('bqhcd,bkhcd->bhcqk', qb, k).astype(jnp.float32) * scale
        dist = jnp.abs(qpos[:, None] - kpos[None, :]).astype(jnp.float32)
        s = s - slopes[:, None, None, None] * dist
        p = jax.nn.softmax(s, axis=-1)
        a = p[:, :, 0] - lam * p[:, :, 1]
        return jnp.einsum('bhqk,bkhd->bqhd', a.astype(v.dtype), v)

    o = _sweep_query_blocks(block, q)
    o = _rms(o, subln_g) * (1.0 - lambda_init)
    return o.reshape(bsz, seq, -1)


def _mla(cq, ckv, k_rope, q_norm_g, w_uq, kv_norm_g, w_ukv, pos):
    bsz, seq = cq.shape[:2]
    q = (_rms(cq, q_norm_g) @ w_uq).reshape(bsz, seq, MLA_HEADS, MLA_NOPE_DIM + MLA_ROPE_DIM)
    q_nope, q_pe = q[..., :MLA_NOPE_DIM], _rope(q[..., MLA_NOPE_DIM:], pos)
    kv = (_rms(ckv, kv_norm_g) @ w_ukv).reshape(bsz, seq, MLA_HEADS, MLA_NOPE_DIM + MLA_V_DIM)
    k_nope, v = kv[..., :MLA_NOPE_DIM], kv[..., MLA_NOPE_DIM:]
    k_pe = _rope(k_rope[:, :, None, :], pos)[:, :, 0, :]
    scale = (MLA_NOPE_DIM + MLA_ROPE_DIM) ** -0.5

    def block(qpos, qn, qp):
        s = (jnp.einsum('bqhd,bkhd->bhqk', qn, k_nope)
             + jnp.einsum('bqhd,bkd->bhqk', qp, k_pe)).astype(jnp.float32) * scale
        p = jax.nn.softmax(s, axis=-1)
        return jnp.einsum('bhqk,bkhd->bqhd', p.astype(v.dtype), v)

    return _sweep_query_blocks(block, q_nope, q_pe).reshape(bsz, seq, -1)


def _gqa_axial(q, k, v, q_norm_g, k_norm_g, row, col):
    bsz, seq = q.shape[:2]
    q = _axial_rope(_rms(q, q_norm_g), row, col)
    k = _axial_rope(_rms(k, k_norm_g), row, col)
    q = q.reshape(bsz, seq, GQA_KV_HEADS, GQA_HEADS // GQA_KV_HEADS, HEAD_DIM)
    scale = HEAD_DIM ** -0.5

    def block(qpos, qb):
        s = jnp.einsum('bqngd,bknd->bngqk', qb, k).astype(jnp.float32) * scale
        p = jax.nn.softmax(s, axis=-1)
        return jnp.einsum('bngqk,bknd->bqngd', p.astype(v.dtype), v)

    return _sweep_query_blocks(block, q).reshape(bsz, seq, -1)


def _dilated_attention(q, k, v, slopes):
    bsz, seq = q.shape[:2]
    pad = max((w // (2 * d)) * d for w, d in DIL_PAIRS)
    kp = jnp.pad(k, ((0, 0), (pad, pad), (0, 0), (0, 0)))
    vp = jnp.pad(v, ((0, 0), (pad, pad), (0, 0), (0, 0)))
    scale = HEAD_DIM ** -0.5

    def block(qpos, qb):
        outs, lses = [], []
        for w, d in DIL_PAIRS:
            r = w // (2 * d)
            offs = jnp.arange(-r, r + 1, dtype=jnp.int32) * d
            kidx = qpos[:, None] + offs[None, :]
            valid = (kidx >= 0) & (kidx < seq)
            kg = jnp.take(kp, kidx + pad, axis=1)
            vg = jnp.take(vp, kidx + pad, axis=1)
            s = jnp.einsum('bqhd,bqkhd->bhqk', qb, kg).astype(jnp.float32) * scale
            s = s - slopes[:, None, None] * jnp.abs(offs).astype(jnp.float32)[None, None, :]
            s = jnp.where(valid, s, NEG_INF)
            lse = jax.nn.logsumexp(s, axis=-1, keepdims=True)
            p = jnp.exp(s - lse)
            outs.append(jnp.einsum('bhqk,bqkhd->bqhd', p.astype(vg.dtype), vg))
            lses.append(lse[..., 0])
        wts = jax.nn.softmax(jnp.stack(lses, -1), axis=-1)
        wts = jnp.transpose(wts, (0, 2, 1, 3)).astype(qb.dtype)
        return jnp.einsum('bqhdn,bqhn->bqhd', jnp.stack(outs, -1), wts)

    return _sweep_query_blocks(block, q).reshape(bsz, seq, -1)


def _token_mixers(h, lambda_init, w_in, w_o, diff_lambda, diff_subln_g, mla_q_norm_g, mla_w_uq,
                  mla_kv_norm_g, mla_w_ukv, gqa_q_norm_g, gqa_k_norm_g, pos, row, col,
                  slopes_a, slopes_d):
    bsz, seq, _ = h.shape
    proj = h @ w_in
    (a_q, a_k, a_v, b_cq, b_ckv, b_kr, c_q, c_k, c_v, d_q, d_k, d_v) = jnp.split(
        proj, _split_points(), axis=-1)
    lf = diff_lambda.astype(jnp.float32)
    lam = jnp.exp(jnp.sum(lf[0] * lf[1])) - jnp.exp(jnp.sum(lf[2] * lf[3])) + lambda_init
    y_a = _diff_attention(a_q.reshape(bsz, seq, DIFF_HEADS, 2, DIFF_QK_DIM),
                          a_k.reshape(bsz, seq, DIFF_HEADS, 2, DIFF_QK_DIM),
                          a_v.reshape(bsz, seq, DIFF_HEADS, 2 * DIFF_QK_DIM),
                          lam, diff_subln_g, lambda_init, slopes_a)
    y_b = _mla(b_cq, b_ckv, b_kr, mla_q_norm_g, mla_w_uq, mla_kv_norm_g, mla_w_ukv, pos)
    y_c = _gqa_axial(c_q.reshape(bsz, seq, GQA_HEADS, HEAD_DIM),
                     c_k.reshape(bsz, seq, GQA_KV_HEADS, HEAD_DIM),
                     c_v.reshape(bsz, seq, GQA_KV_HEADS, HEAD_DIM),
                     gqa_q_norm_g, gqa_k_norm_g, row, col)
    y_d = _dilated_attention(d_q.reshape(bsz, seq, DIL_HEADS, HEAD_DIM),
                             d_k.reshape(bsz, seq, DIL_HEADS, HEAD_DIM),
                             d_v.reshape(bsz, seq, DIL_HEADS, HEAD_DIM), slopes_d)
    return jnp.concatenate([y_a, y_b, y_c, y_d], axis=-1) @ w_o


def setup_inputs(seed: int = 0) -> dict:
    key = jax.random.key(seed)
    ks = jax.random.split(key, 22)
    beta = (8 * DEPTH) ** -0.25
    L = DEPTH
    in_cols = sum(_column_widths())

    def nrm(k, shape, scale):
        return jax.random.normal(k, shape, jnp.float32) * scale

    return dict(
        x=nrm(ks[0], (BATCH, SEQ, D_MODEL), 1.0),
        c=nrm(ks[1], (BATCH, D_MODEL), 1.0),
        w_ada=nrm(ks[2], (L, D_MODEL, 6 * D_MODEL), 0.5 * D_MODEL ** -0.5),
        b_ada=nrm(ks[3], (L, 6 * D_MODEL), 0.02),
        w_in=nrm(ks[4], (L, D_MODEL, in_cols), D_MODEL ** -0.5),
        w_o=nrm(ks[5], (L, MIX_WIDTH, D_MODEL), beta * MIX_WIDTH ** -0.5),
        diff_lambda=nrm(ks[6], (L, 4, DIFF_QK_DIM), 0.1),
        diff_subln_g=1.0 + nrm(ks[7], (L, 2 * DIFF_QK_DIM), 0.02),
        mla_q_norm_g=1.0 + nrm(ks[8], (L, MLA_Q_RANK), 0.02),
        mla_w_uq=nrm(ks[9], (L, MLA_Q_RANK, MLA_HEADS * (MLA_NOPE_DIM + MLA_ROPE_DIM)), MLA_Q_RANK ** -0.5),
        mla_kv_norm_g=1.0 + nrm(ks[10], (L, MLA_KV_RANK), 0.02),
        mla_w_ukv=nrm(ks[11], (L, MLA_KV_RANK, MLA_HEADS * (MLA_NOPE_DIM + MLA_V_DIM)), MLA_KV_RANK ** -0.5),
        gqa_q_norm_g=1.0 + nrm(ks[12], (L, HEAD_DIM), 0.02),
        gqa_k_norm_g=1.0 + nrm(ks[13], (L, HEAD_DIM), 0.02),
        ln_attn_g=1.0 + nrm(ks[14], (L, D_MODEL), 0.02),
        ln_attn_b=nrm(ks[15], (L, D_MODEL), 0.02),
        w_up=nrm(ks[16], (L, D_MODEL, MLP_HIDDEN), D_MODEL ** -0.5),
        w_down=nrm(ks[17], (L, MLP_HIDDEN, D_MODEL), beta * MLP_HIDDEN ** -0.5),
        ln_mlp_g=1.0 + nrm(ks[18], (L, D_MODEL), 0.02),
        ln_mlp_b=nrm(ks[19], (L, D_MODEL), 0.02),
    )


def reference(x, c, w_ada, b_ada, w_in, w_o, diff_lambda, diff_subln_g, mla_q_norm_g, mla_w_uq,
              mla_kv_norm_g, mla_w_ukv, gqa_q_norm_g, gqa_k_norm_g, ln_attn_g, ln_attn_b,
              w_up, w_down, ln_mlp_g, ln_mlp_b):
    alpha = (2 * DEPTH) ** 0.25
    seq = x.shape[1]
    rows = seq // GRID_W
    pos = jnp.arange(seq, dtype=jnp.float32)
    row = jnp.repeat(jnp.arange(rows, dtype=jnp.float32), GRID_W)
    col = jnp.tile(jnp.arange(GRID_W, dtype=jnp.float32), rows)
    slopes = _alibi_slopes(DIFF_HEADS + DIL_HEADS)
    slopes_a, slopes_d = slopes[0::2], slopes[1::2]
    cond = jax.nn.silu(c)
    for l in range(DEPTH):
        lambda_init = 0.8 - 0.6 * math.exp(-0.3 * l)
        mod = cond @ w_ada[l] + b_ada[l]
        sh_a, sc_a, g_a, sh_m, sc_m, g_m = [m[:, None, :] for m in jnp.split(mod, 6, axis=-1)]
        h = x * (1.0 + sc_a) + sh_a
        y = _token_mixers(h, lambda_init, w_in[l], w_o[l], diff_lambda[l], diff_subln_g[l],
                          mla_q_norm_g[l], mla_w_uq[l], mla_kv_norm_g[l], mla_w_ukv[l],
                          gqa_q_norm_g[l], gqa_k_norm_g[l], pos, row, col, slopes_a, slopes_d)
        x = _layer_norm(alpha * x + g_a * y, ln_attn_g[l], ln_attn_b[l])
        h = x * (1.0 + sc_m) + sh_m
        u = jnp.square(jax.nn.relu(h @ w_up[l])) @ w_down[l]
        x = _layer_norm(alpha * x + g_m * u, ln_mlp_g[l], ln_mlp_b[l])
    return x
```

```python
import functools
import math

import jax
import jax.numpy as jnp
import numpy as np
from jax import lax
from jax.experimental import pallas as pl
from jax.experimental.pallas import tpu as pltpu

D_MODEL = 1024
DEPTH = 2
HEAD_DIM = 64
GROUP_WIDTH = D_MODEL // 4
N_HEADS = GROUP_WIDTH // HEAD_DIM
DIFF_QK_DIM = HEAD_DIM // 2
MLA_NOPE_DIM = HEAD_DIM
MLA_ROPE_DIM = HEAD_DIM // 2
MLA_V_DIM = HEAD_DIM
MLA_QK_DIM = MLA_NOPE_DIM + MLA_ROPE_DIM
MLA_Q_RANK = N_HEADS * MLA_QK_DIM
MLA_KV_RANK = 4 * MLA_V_DIM
GQA_KV_HEADS = N_HEADS // 2
GRID_W = 64
DIL_PAIRS = ((128, 1), (512, 4), (2048, 16))
MLP_HIDDEN = 4 * D_MODEL
ROPE_THETA = 10000.0
RMS_EPS = 1e-6
LN_EPS = 1e-5
NEG_INF = -1e30

LANES = 128
VMEM_LIMIT_BYTES = 56 * 1024 * 1024

PROJ_ROWS = 256
ATT_TQ = 512
ATT_TK = 512
ROW_TILE = 512
MLP_CHUNK = 1024
AUG_LANE = DIFF_QK_DIM
POS_SPLIT = 64
DIL_BAND = 1024

_NT = (((1,), (1,)), ((), ()))

F32 = jnp.float32
BF16 = jnp.bfloat16


def _params(**kw):
    return pltpu.CompilerParams(vmem_limit_bytes=VMEM_LIMIT_BYTES, **kw)


def _const_spec(shape):
    nd = len(shape)
    return pl.BlockSpec(shape, lambda *_: (0,) * nd, pipeline_mode=pl.Buffered(1))


def _ada_kernel(c_ref, w_ref, b_ref, o_ref):
    c = c_ref[...]
    cond = c * (1.0 / (1.0 + jnp.exp(-c)))
    o_ref[0, 0] = jnp.dot(cond, w_ref[0], preferred_element_type=F32) + b_ref[0, 0]


def _ada_call(c, w_ada, b_ada):
    n_layers, d, _ = w_ada.shape
    bsz = c.shape[0]
    b4 = b_ada.reshape(n_layers, 6, 1, d)
    out = pl.pallas_call(
        _ada_kernel,
        out_shape=jax.ShapeDtypeStruct((n_layers, 6, bsz, d), F32),
        grid=(n_layers, 6),
        in_specs=[
            pl.BlockSpec((bsz, d), lambda l, j: (0, 0)),
            pl.BlockSpec((1, d, d), lambda l, j: (l, 0, j)),
            pl.BlockSpec((1, 1, 1, d), lambda l, j: (l, j, 0, 0)),
        ],
        out_specs=pl.BlockSpec((1, 1, bsz, d), lambda l, j: (l, j, 0, 0)),
        compiler_params=_params(),
        name="adaln_mod",
    )(c, w_ada, b4)
    return jnp.transpose(out, (0, 2, 1, 3))


_SEG_WIDTHS = (
    ("a_q", 256), ("a_k", 256), ("b_cq", 384), ("b_ckv", 256), ("b_kr", 128),
    ("c_q", 256), ("c_q_sw", 256), ("c_k", 128), ("c_k_sw", 128),
    ("d_q", 256), ("d_k", 256), ("v_a", 512), ("v_c", 256), ("v_d", 512),
)
_SEG = {}
_off = 0
for _name, _w in _SEG_WIDTHS:
    _SEG[_name] = (_off, _off + _w)
    _off += _w
W1_COLS = _off


def _rms_rows(x, gsum):
    sq = x * x
    hi = sq.astype(BF16)
    lo = (sq - hi.astype(F32)).astype(BF16)
    ms = (jnp.dot(hi, gsum, preferred_element_type=F32)
          + jnp.dot(lo, gsum, preferred_element_type=F32))
    return lax.rsqrt(ms + RMS_EPS)


def _proj_kernel(x_ref, mod_ref, w1_ref, pa_ref, wuq_ref, wukv_ref, e_ref, g_ref, pcq_ref, pck_ref,
                 vec_ref, tb_ref, tc_ref, tck_ref, tkr_ref, qaug_ref, kaug_ref,
                 qa_ref, ka_ref, va_ref, qb_ref, kb_ref, vb_ref, qc_ref, kc_ref, vc_ref,
                 qd_ref, kd_ref, vd_ref):
    sh = mod_ref[0, 0:1, :]
    sc = mod_ref[0, 1:2, :]
    h = (x_ref[0] * (1.0 + sc) + sh).astype(BF16)

    def seg(name):
        a, b = _SEG[name]
        return jnp.dot(h, w1_ref[:, a:b], preferred_element_type=F32)

    def vec(row, width):
        return vec_ref[row:row + 1, 0:width]

    a_scale = DIFF_QK_DIM ** -0.5
    qa = jnp.dot((seg("a_q") * a_scale).astype(BF16), pa_ref[...], preferred_element_type=F32)
    qa_ref[0] = qa.astype(BF16) + qaug_ref[...]
    ka = jnp.dot(seg("a_k").astype(BF16), pa_ref[...], preferred_element_type=F32)
    ka_ref[0] = ka.astype(BF16) + kaug_ref[...]
    va_ref[0] = (seg("v_a") + vec(0, 512)).astype(BF16)

    cq = seg("b_cq")
    r = lax.rsqrt(jnp.mean(cq * cq, axis=-1, keepdims=True) + RMS_EPS)
    cqn = (cq * r * vec(1, 384)).astype(BF16)
    q2 = jnp.dot(cqn, wuq_ref[...], preferred_element_type=F32)
    qb_ref[0] = (q2[:, 0:512] * tb_ref[0] + q2[:, 512:1024] * tb_ref[1]).astype(BF16)
    ckv = seg("b_ckv")
    r = lax.rsqrt(jnp.mean(ckv * ckv, axis=-1, keepdims=True) + RMS_EPS)
    ckvn = (ckv * r * vec(2, 256)).astype(BF16)
    kv2 = jnp.dot(ckvn, wukv_ref[...], preferred_element_type=F32)
    kr = (seg("b_kr") * tkr_ref[...]).astype(BF16)
    kb_ref[0] = (kv2[:, 0:512] + jnp.dot(kr, e_ref[...], preferred_element_type=F32)).astype(BF16)
    vb_ref[0] = (kv2[:, 512:1024] + vec(0, 512)).astype(BF16)

    cqx = seg("c_q")
    r = _rms_rows(cqx, g_ref[...])
    qc = cqx * r * vec(3, 256) * tc_ref[0] + seg("c_q_sw") * r * vec(4, 256) * tc_ref[1]
    qc_ref[0] = jnp.dot(qc.astype(BF16), pcq_ref[...], preferred_element_type=F32).astype(BF16)
    ckx = seg("c_k")
    r = _rms_rows(ckx, g_ref[0:128, 0:128])
    kc = ckx * r * vec(5, 128) * tck_ref[0] + seg("c_k_sw") * r * vec(6, 128) * tck_ref[1]
    kc_ref[0] = jnp.dot(kc.astype(BF16), pck_ref[...], preferred_element_type=F32).astype(BF16)
    vc_ref[0] = (seg("v_c") + vec(0, 256)).astype(BF16)

    d_scale = HEAD_DIM ** -0.5
    qd = jnp.dot((seg("d_q") * d_scale).astype(BF16), pcq_ref[...], preferred_element_type=F32)
    qd_ref[0] = qd.astype(BF16)
    kd_ref[0] = jnp.dot(seg("d_k").astype(BF16), pcq_ref[...], preferred_element_type=F32).astype(BF16)
    vd_ref[0] = (seg("v_d") + vec(0, 512)).astype(BF16)


def _proj_call(x, mod_l, w1, consts, layer_w, tables):
    bsz, seq, d = x.shape
    tm = PROJ_ROWS
    nt = seq // tm
    out_widths = (1024, 1024, 512, 512, 512, 512, 512, 256, 256, 512, 512, 512)
    row_spec = lambda w: pl.BlockSpec((1, tm, w), lambda b, i: (b, i, 0))
    tab3 = lambda w: pl.BlockSpec((2, tm, w), lambda b, i: (0, i, 0))
    tab2 = lambda w: pl.BlockSpec((tm, w), lambda b, i: (i, 0))
    in_specs = [
        row_spec(d),
        pl.BlockSpec((1, 6, d), lambda b, i: (b, 0, 0)),
        _const_spec(w1.shape),
        _const_spec(consts["pa"].shape),
        _const_spec(layer_w["wuq"].shape),
        _const_spec(layer_w["wukv"].shape),
        _const_spec(consts["e"].shape),
        _const_spec(consts["g"].shape),
        _const_spec(consts["pcq"].shape),
        _const_spec(consts["pck"].shape),
        _const_spec(layer_w["vec"].shape),
        tab3(512), tab3(256), tab3(128), tab2(128), tab2(1024), tab2(1024),
    ]
    return pl.pallas_call(
        _proj_kernel,
        out_shape=tuple(jax.ShapeDtypeStruct((bsz, seq, w), BF16) for w in out_widths),
        grid=(bsz, nt),
        in_specs=in_specs,
        out_specs=tuple(row_spec(w) for w in out_widths),
        compiler_params=_params(),
        name="proj_prep",
    )(x, mod_l, w1, consts["pa"], layer_w["wuq"], layer_w["wukv"], consts["e"], consts["g"],
      consts["pcq"], consts["pck"], layer_w["vec"], tables["tb"], tables["tc"], tables["tck"],
      tables["tkr"], tables["qaug"], tables["kaug"])


def _softmax_step(s, v, m, acc):
    m_new = jnp.maximum(m, jnp.max(s, axis=-1, keepdims=True))
    p = jnp.exp(s - m_new)
    alpha = jnp.exp(m - m_new)
    acc = alpha * acc + jnp.dot(p.astype(BF16), v, preferred_element_type=F32)
    return m_new, acc


def _softmax_init(tq):
    return jnp.full((tq, 1), NEG_INF, F32), jnp.zeros((tq, LANES), F32)


def _normalize(acc):
    return acc[:, 0:HEAD_DIM] / acc[:, HEAD_DIM:HEAD_DIM + 1]


def _attn_plain_kernel(q_ref, k_ref, v_ref, o_ref, *, n_heads, q_per_kv):
    tq = q_ref.shape[1]
    nk = k_ref.shape[1] // ATT_TK
    outs = []
    for h in range(n_heads):
        g = h // q_per_kv
        q = q_ref[0, :, LANES * h:LANES * (h + 1)]

        def body(j, carry, q=q, g=g):
            off = pl.multiple_of(j * ATT_TK, ATT_TK)
            k = k_ref[0, pl.ds(off, ATT_TK), LANES * g:LANES * (g + 1)]
            v = v_ref[0, pl.ds(off, ATT_TK), LANES * g:LANES * (g + 1)]
            s = lax.dot_general(q, k, _NT, preferred_element_type=F32)
            return _softmax_step(s, v, *carry)

        _, acc = lax.fori_loop(0, nk, body, _softmax_init(tq))
        outs.append(_normalize(acc))
    o_ref[0] = jnp.concatenate(outs, axis=1).astype(o_ref.dtype)


def _attn_plain_call(q, k, v, *, n_heads, q_per_kv):
    bsz, seq, qw = q.shape
    kw = k.shape[2]
    kern = functools.partial(_attn_plain_kernel, n_heads=n_heads, q_per_kv=q_per_kv)
    return pl.pallas_call(
        kern,
        out_shape=jax.ShapeDtypeStruct((bsz, seq, n_heads * HEAD_DIM), BF16),
        grid=(bsz, seq // ATT_TQ),
        in_specs=[
            pl.BlockSpec((1, ATT_TQ, qw), lambda b, i: (b, i, 0)),
            pl.BlockSpec((1, seq, kw), lambda b, i: (b, 0, 0)),
            pl.BlockSpec((1, seq, kw), lambda b, i: (b, 0, 0)),
        ],
        out_specs=pl.BlockSpec((1, ATT_TQ, n_heads * HEAD_DIM), lambda b, i: (b, i, 0)),
        compiler_params=_params(),
        name="attn_plain",
    )(q, k, v)


def _attn_diff_kernel(q_ref, k_ref, v_ref, lam_ref, g_ref, o_ref, *, slopes, lambda_init):
    tq = q_ref.shape[1]
    nk = k_ref.shape[1] // ATT_TK
    qi = pl.program_id(1)
    lane = lax.broadcasted_iota(jnp.int32, (1, LANES), 1)
    keep = jnp.where(lane < AUG_LANE, 1.0, 0.0).astype(BF16)
    flip = jnp.where(lane < AUG_LANE, 1.0, -1.0).astype(BF16)
    rows = lax.broadcasted_iota(jnp.int32, (tq, ATT_TK), 0)
    cols = lax.broadcasted_iota(jnp.int32, (tq, ATT_TK), 1)
    dist = jnp.abs(rows - cols).astype(F32)

    lf = lam_ref[...]
    lam = (jnp.exp(jnp.sum(lf[0:1] * lf[1:2], axis=-1, keepdims=True))
           - jnp.exp(jnp.sum(lf[2:3] * lf[3:4], axis=-1, keepdims=True)) + lambda_init)

    outs = []
    for h in range(N_HEADS):
        comps = []
        for c in range(2):
            hc = 2 * h + c
            q = q_ref[0, :, LANES * hc:LANES * (hc + 1)]

            def chunk(j, hc=hc, h=h):
                off = pl.multiple_of(j * ATT_TK, ATT_TK)
                k = k_ref[0, pl.ds(off, ATT_TK), LANES * hc:LANES * (hc + 1)]
                v = v_ref[0, pl.ds(off, ATT_TK), LANES * h:LANES * (h + 1)]
                return k, v

            def body(j, carry, qv=None):
                k, v = chunk(j)
                s = lax.dot_general(qv, k, _NT, preferred_element_type=F32)
                return _softmax_step(s, v, *carry)

            carry = lax.fori_loop(0, qi, functools.partial(body, qv=q), _softmax_init(tq))
            k, v = chunk(qi)
            s = lax.dot_general(q * keep, k, _NT, preferred_element_type=F32) - slopes[h] * dist
            carry = _softmax_step(s, v, *carry)
            _, acc = lax.fori_loop(qi + 1, nk, functools.partial(body, qv=q * flip), carry)
            comps.append(_normalize(acc))
        y = comps[0] - lam * comps[1]
        r = lax.rsqrt(jnp.mean(y * y, axis=-1, keepdims=True) + RMS_EPS)
        outs.append(y * r * g_ref[...] * (1.0 - lambda_init))
    o_ref[0] = jnp.concatenate(outs, axis=1).astype(o_ref.dtype)


def _attn_diff_call(q, k, v, lam, subln_g, *, slopes, lambda_init):
    bsz, seq, qw = q.shape
    vw = v.shape[2]
    kern = functools.partial(_attn_diff_kernel, slopes=slopes, lambda_init=lambda_init)
    return pl.pallas_call(
        kern,
        out_shape=jax.ShapeDtypeStruct((bsz, seq, GROUP_WIDTH), BF16),
        grid=(bsz, seq // ATT_TQ),
        in_specs=[
            pl.BlockSpec((1, ATT_TQ, qw), lambda b, i: (b, i, 0)),
            pl.BlockSpec((1, seq, qw), lambda b, i: (b, 0, 0)),
            pl.BlockSpec((1, seq, vw), lambda b, i: (b, 0, 0)),
            _const_spec(lam.shape),
            _const_spec(subln_g.shape),
        ],
        out_specs=pl.BlockSpec((1, ATT_TQ, GROUP_WIDTH), lambda b, i: (b, i, 0)),
        compiler_params=_params(),
        name="attn_diff",
    )(q, k, v, lam, subln_g)


def _attn_dil_kernel(q_ref, k_ref, v_ref, bias_ref, o_ref):
    tq = q_ref.shape[1]
    nk = k_ref.shape[1] // ATT_TK
    reach = DIL_BAND // ATT_TK
    qi = pl.program_id(1)
    t_lo = jnp.maximum(0, reach - qi)
    t_hi = jnp.minimum(2 * reach + 1, nk - qi + reach)
    outs = []
    for h in range(N_HEADS):
        q = q_ref[0, :, LANES * h:LANES * (h + 1)]

        def body(t, carry, q=q, h=h):
            off = pl.multiple_of((qi + t - reach) * ATT_TK, ATT_TK)
            k = k_ref[0, pl.ds(off, ATT_TK), LANES * h:LANES * (h + 1)]
            v = v_ref[0, pl.ds(off, ATT_TK), LANES * h:LANES * (h + 1)]
            s = lax.dot_general(q, k, _NT, preferred_element_type=F32) + bias_ref[h, t]
            return _softmax_step(s, v, *carry)

        _, acc = lax.fori_loop(t_lo, t_hi, body, _softmax_init(tq))
        outs.append(_normalize(acc))
    o_ref[0] = jnp.concatenate(outs, axis=1).astype(o_ref.dtype)


def _attn_dil_call(q, k, v, bias):
    bsz, seq, qw = q.shape
    return pl.pallas_call(
        _attn_dil_kernel,
        out_shape=jax.ShapeDtypeStruct((bsz, seq, GROUP_WIDTH), BF16),
        grid=(bsz, seq // ATT_TQ),
        in_specs=[
            pl.BlockSpec((1, ATT_TQ, qw), lambda b, i: (b, i, 0)),
            pl.BlockSpec((1, seq, qw), lambda b, i: (b, 0, 0)),
            pl.BlockSpec((1, seq, qw), lambda b, i: (b, 0, 0)),
            _const_spec(bias.shape),
        ],
        out_specs=pl.BlockSpec((1, ATT_TQ, GROUP_WIDTH), lambda b, i: (b, i, 0)),
        compiler_params=_params(),
        name="attn_dilated",
    )(q, k, v, bias)


def _layer_norm(z, g, b):
    mu = jnp.mean(z, axis=-1, keepdims=True)
    zc = z - mu
    var = jnp.mean(zc * zc, axis=-1, keepdims=True)
    return zc * lax.rsqrt(var + LN_EPS) * g + b


def _outproj_kernel(ya_ref, yb_ref, yc_ref, yd_ref, x_ref, mod_ref, wo_ref, ln_ref, o_ref, *, alpha):
    ycat = jnp.concatenate([ya_ref[0], yb_ref[0], yc_ref[0], yd_ref[0]], axis=1)
    y = jnp.dot(ycat, wo_ref[...], preferred_element_type=F32)
    z = alpha * x_ref[0] + mod_ref[0, 2:3, :] * y
    o_ref[0] = _layer_norm(z, ln_ref[0:1, :], ln_ref[1:2, :])


def _outproj_call(ys, x, mod_l, wo, ln, *, alpha):
    bsz, seq, d = x.shape
    tm = ROW_TILE
    yspec = pl.BlockSpec((1, tm, GROUP_WIDTH), lambda b, i: (b, i, 0))
    xspec = pl.BlockSpec((1, tm, d), lambda b, i: (b, i, 0))
    return pl.pallas_call(
        functools.partial(_outproj_kernel, alpha=alpha),
        out_shape=jax.ShapeDtypeStruct((bsz, seq, d), F32),
        grid=(bsz, seq // tm),
        in_specs=[yspec, yspec, yspec, yspec, xspec,
                  pl.BlockSpec((1, 6, d), lambda b, i: (b, 0, 0)),
                  _const_spec(wo.shape), _const_spec(ln.shape)],
        out_specs=xspec,
        compiler_params=_params(),
        name="out_proj_ln",
    )(*ys, x, mod_l, wo, ln)


def _mlp_kernel(x_ref, mod_ref, wup_ref, wdn_ref, ln_ref, o_ref, *, alpha):
    x = x_ref[0]
    h = (x * (1.0 + mod_ref[0, 4:5, :]) + mod_ref[0, 3:4, :]).astype(BF16)
    u = jnp.zeros(x.shape, F32)
    for c in range(MLP_HIDDEN // MLP_CHUNK):
        a = jnp.dot(h, wup_ref[:, c * MLP_CHUNK:(c + 1) * MLP_CHUNK], preferred_element_type=F32)
        a = jnp.maximum(a, 0.0)
        u = u + jnp.dot((a * a).astype(BF16), wdn_ref[c * MLP_CHUNK:(c + 1) * MLP_CHUNK, :],
                        preferred_element_type=F32)
    z = alpha * x + mod_ref[0, 5:6, :] * u
    o_ref[0] = _layer_norm(z, ln_ref[0:1, :], ln_ref[1:2, :])


def _mlp_call(x, mod_l, wup, wdn, ln, *, alpha):
    bsz, seq, d = x.shape
    tm = ROW_TILE
    xspec = pl.BlockSpec((1, tm, d), lambda b, i: (b, i, 0))
    return pl.pallas_call(
        functools.partial(_mlp_kernel, alpha=alpha),
        out_shape=jax.ShapeDtypeStruct((bsz, seq, d), F32),
        grid=(bsz, seq // tm),
        in_specs=[xspec, pl.BlockSpec((1, 6, d), lambda b, i: (b, 0, 0)),
                  _const_spec(wup.shape), _const_spec(wdn.shape), _const_spec(ln.shape)],
        out_specs=xspec,
        compiler_params=_params(),
        name="mlp_ln",
    )(x, mod_l, wup, wdn, ln)


def _rot_half_source(width, block):
    half = block // 2
    d = np.arange(width)
    inner = d % block
    src = np.where(inner < half, d + half, d - half)
    sign = np.where(inner < half, -1.0, 1.0)
    return src, sign.astype(np.float32)


def _placement(n_in, n_out, pairs):
    m = np.zeros((n_in, n_out), np.float32)
    for r, c in pairs:
        m[r, c] = 1.0
    return jnp.asarray(m, BF16)


def _static_consts():
    pa = _placement(256, 1024, [(32 * hc + d, LANES * hc + d) for hc in range(8) for d in range(32)])
    pcq = _placement(256, 512, [(64 * h + d, LANES * h + d) for h in range(4) for d in range(64)])
    pck = _placement(128, 256, [(64 * g + d, LANES * g + d) for g in range(2) for d in range(64)])
    e = _placement(128, 512, [(s * 32 + d, LANES * h + MLA_NOPE_DIM + d)
                              for h in range(4) for s in range(2) for d in range(32)])
    g = np.zeros((256, 256), np.float32)
    for h in range(4):
        g[64 * h:64 * (h + 1), 64 * h:64 * (h + 1)] = 1.0 / HEAD_DIM
    return dict(pa=pa, pcq=pcq, pck=pck, e=e, g=jnp.asarray(g, BF16))


def _rope_tables(pos, half):
    freqs = ROPE_THETA ** (-jnp.arange(half, dtype=F32) / half)
    ang = pos[:, None] * freqs[None, :]
    cos = jnp.concatenate([jnp.cos(ang), jnp.cos(ang)], axis=1)
    sin = jnp.concatenate([jnp.sin(ang), jnp.sin(ang)], axis=1)
    return cos, sin


def _position_tables(seq, slopes_a, slopes_d):
    pos = jnp.arange(seq, dtype=F32)
    row = jnp.floor(pos / GRID_W)
    col = pos - row * GRID_W
    cos32, sin32 = _rope_tables(pos, MLA_ROPE_DIM // 2)
    zeros32 = jnp.zeros((seq, 32), F32)
    ones64 = jnp.ones((seq, 64), F32)
    zeros64 = jnp.zeros((seq, 64), F32)
    b_scale = MLA_QK_DIM ** -0.5
    cos_b = jnp.tile(jnp.concatenate([ones64, cos32, zeros32], 1), (1, 4)) * b_scale
    sin_b = jnp.tile(jnp.concatenate([zeros64, sin32, zeros32], 1), (1, 4)) * b_scale
    tkr = jnp.concatenate([cos32, sin32, zeros64], 1)
    cr, sr = _rope_tables(row, HEAD_DIM // 4)
    cc, sc = _rope_tables(col, HEAD_DIM // 4)
    cos_h = jnp.concatenate([cr, cc], 1)
    sin_h = jnp.concatenate([sr, sc], 1)
    c_scale = HEAD_DIM ** -0.5
    tc = jnp.stack([jnp.tile(cos_h, (1, 4)), jnp.tile(sin_h, (1, 4))]) * c_scale
    tck = jnp.stack([jnp.tile(cos_h, (1, 2)), jnp.tile(sin_h, (1, 2))])
    ipos = np.arange(seq)
    hi = (ipos // POS_SPLIT).astype(np.float32)
    lo = (ipos % POS_SPLIT).astype(np.float32)
    qaug = np.zeros((seq, 8 * LANES), np.float32)
    kaug = np.zeros((seq, 8 * LANES), np.float32)
    for hc in range(8):
        s = float(slopes_a[hc // 2])
        base = LANES * hc + AUG_LANE
        qaug[:, base + 0] = hi
        qaug[:, base + 1] = lo
        qaug[:, base + 2] = 1.0
        qaug[:, base + 3] = 1.0
        kaug[:, base + 0] = -s * POS_SPLIT
        kaug[:, base + 1] = -s
        kaug[:, base + 2] = s * POS_SPLIT * hi
        kaug[:, base + 3] = s * lo
    reach = DIL_BAND // ATT_TK
    a = np.arange(ATT_TQ)[:, None]
    b = np.arange(ATT_TK)[None, :]
    tabs = []
    for t in range(2 * reach + 1):
        o = (t - reach) * ATT_TK + b - a
        cnt = np.zeros(o.shape, np.float32)
        for w, dil in DIL_PAIRS:
            cnt += ((np.abs(o) <= w // 2) & (o % dil == 0)).astype(np.float32)
        tabs.append((np.abs(o).astype(np.float32), cnt))
    dist = np.stack([t[0] for t in tabs])
    cnt = np.stack([t[1] for t in tabs])
    logc = np.where(cnt > 0, np.log(np.maximum(cnt, 1.0)), NEG_INF).astype(np.float32)
    bias_d = np.stack([np.where(cnt > 0, -float(s) * dist + logc, NEG_INF) for s in slopes_d])
    return dict(tb=jnp.stack([cos_b, sin_b]), tc=tc, tck=tck, tkr=tkr,
                qaug=jnp.asarray(qaug, BF16), kaug=jnp.asarray(kaug, BF16),
                bias_d=jnp.asarray(bias_d.astype(np.float32)))


def _place_cols(w, n_heads, src_stride, src_off, width, dst_width=LANES):
    parts = []
    for h in range(n_heads):
        blk = w[:, h * src_stride + src_off:h * src_stride + src_off + width]
        parts.append(jnp.pad(blk, ((0, 0), (0, dst_width - width))))
    return jnp.concatenate(parts, axis=1)


def _layer_weights(w_in, mla_w_uq, mla_w_ukv, mla_q_norm_g, mla_kv_norm_g, gqa_q_norm_g, gqa_k_norm_g):
    o_aq, o_ak, o_av, o_cq, o_ckv, o_kr = 0, 256, 512, 768, 1152, 1408
    o_gq, o_gk, o_gv, o_dq, o_dk, o_dv = 1440, 1696, 1824, 1952, 2208, 2464
    src32, sign32 = _rot_half_source(32, 32)
    kr = w_in[:, o_kr:o_kr + 32]
    kr_sw = kr[:, src32] * sign32
    srcq, signq = _rot_half_source(256, 32)
    gq = w_in[:, o_gq:o_gq + 256]
    gk = w_in[:, o_gk:o_gk + 128]
    w1 = jnp.concatenate([
        w_in[:, o_aq:o_aq + 256], w_in[:, o_ak:o_ak + 256],
        w_in[:, o_cq:o_cq + 384], w_in[:, o_ckv:o_ckv + 256],
        kr, kr_sw, jnp.zeros((w_in.shape[0], 64), w_in.dtype),
        gq, gq[:, srcq] * signq, gk, gk[:, srcq[:128]] * signq[:128],
        w_in[:, o_dq:o_dq + 256], w_in[:, o_dk:o_dk + 256],
        _place_cols(w_in[:, o_av:o_av + 256], 4, 64, 0, 64),
        _place_cols(w_in[:, o_gv:o_gv + 128], 2, 64, 0, 64),
        _place_cols(w_in[:, o_dv:o_dv + 256], 4, 64, 0, 64),
    ], axis=1).astype(BF16)
    assert w1.shape[1] == W1_COLS
    uq_plain = _place_cols(mla_w_uq, 4, MLA_QK_DIM, 0, MLA_QK_DIM)
    pe = [mla_w_uq[:, h * MLA_QK_DIM + MLA_NOPE_DIM:(h + 1) * MLA_QK_DIM] for h in range(4)]
    uq_sw = jnp.concatenate([
        jnp.pad(p[:, src32] * sign32, ((0, 0), (MLA_NOPE_DIM, LANES - MLA_QK_DIM))) for p in pe], axis=1)
    wuq = jnp.concatenate([uq_plain, uq_sw], axis=1).astype(BF16)
    uk = _place_cols(mla_w_ukv, 4, MLA_NOPE_DIM + MLA_V_DIM, 0, MLA_NOPE_DIM)
    uv = _place_cols(mla_w_ukv, 4, MLA_NOPE_DIM + MLA_V_DIM, MLA_NOPE_DIM, MLA_V_DIM)
    wukv = jnp.concatenate([uk, uv], axis=1).astype(BF16)
    ones = np.zeros((512,), np.float32)
    ones[HEAD_DIM::LANES] = 1.0
    src64, _ = _rot_half_source(64, 32)
    gq_g = jnp.tile(gqa_q_norm_g, 4)
    gq_gs = jnp.tile(gqa_q_norm_g[src64], 4)
    gk_g = jnp.tile(gqa_k_norm_g, 2)
    gk_gs = jnp.tile(gqa_k_norm_g[src64], 2)
    pad = lambda v: jnp.pad(v.astype(F32), (0, 512 - v.shape[0]))
    vec = jnp.stack([jnp.asarray(ones), pad(mla_q_norm_g), pad(mla_kv_norm_g),
                     pad(gq_g), pad(gq_gs), pad(gk_g), pad(gk_gs), jnp.zeros((512,), F32)])
    return w1, dict(wuq=wuq, wukv=wukv, vec=vec)


def kernel(x, c, w_ada, b_ada, w_in, w_o, diff_lambda, diff_subln_g, mla_q_norm_g, mla_w_uq, mla_kv_norm_g, mla_w_ukv, gqa_q_norm_g, gqa_k_norm_g, ln_attn_g, ln_attn_b, w_up, w_down, ln_mlp_g, ln_mlp_b):
    alpha = (2 * DEPTH) ** 0.25
    seq = x.shape[1]
    n_slopes = 2 * N_HEADS
    slopes = [2.0 ** (-8.0 * (n + 1.0) / n_slopes) for n in range(n_slopes)]
    slopes_a, slopes_d = tuple(slopes[0::2]), tuple(slopes[1::2])
    consts = _static_consts()
    tables = _position_tables(seq, slopes_a, slopes_d)
    mod = _ada_call(c, w_ada, b_ada)
    for l in range(DEPTH):
        lambda_init = 0.8 - 0.6 * math.exp(-0.3 * l)
        w1, layer_w = _layer_weights(w_in[l], mla_w_uq[l], mla_w_ukv[l], mla_q_norm_g[l],
                                     mla_kv_norm_g[l], gqa_q_norm_g[l], gqa_k_norm_g[l])
        (qa, ka, va, qb, kb, vb, qc, kc, vc, qd, kd, vd) = _proj_call(
            x, mod[l], w1, consts, layer_w, tables)
        ya = _attn_diff_call(qa, ka, va, diff_lambda[l], diff_subln_g[l].reshape(1, HEAD_DIM),
                             slopes=slopes_a, lambda_init=lambda_init)
        yb = _attn_plain_call(qb, kb, vb, n_heads=N_HEADS, q_per_kv=1)
        yc = _attn_plain_call(qc, kc, vc, n_heads=N_HEADS, q_per_kv=N_HEADS // GQA_KV_HEADS)
        yd = _attn_dil_call(qd, kd, vd, tables["bias_d"])
        x = _outproj_call((ya, yb, yc, yd), x, mod[l], w_o[l].astype(BF16),
                          jnp.stack([ln_attn_g[l], ln_attn_b[l]]), alpha=alpha)
        x = _mlp_call(x, mod[l], w_up[l].astype(BF16), w_down[l].astype(BF16),
                      jnp.stack([ln_mlp_g[l], ln_mlp_b[l]]), alpha=alpha)
    return x
```

```python
import functools
import math

import jax
import jax.numpy as jnp
import numpy as np
from jax import lax
from jax.experimental import pallas as pl
from jax.experimental.pallas import tpu as pltpu

D_MODEL = 1024
DEPTH = 2
HEAD_DIM = 64
GROUP_WIDTH = D_MODEL // 4
N_HEADS = GROUP_WIDTH // HEAD_DIM
DIFF_QK_DIM = HEAD_DIM // 2
MLA_NOPE_DIM = HEAD_DIM
MLA_ROPE_DIM = HEAD_DIM // 2
MLA_V_DIM = HEAD_DIM
MLA_QK_DIM = MLA_NOPE_DIM + MLA_ROPE_DIM
MLA_Q_RANK = N_HEADS * MLA_QK_DIM
MLA_KV_RANK = 4 * MLA_V_DIM
GQA_KV_HEADS = N_HEADS // 2
GRID_W = 64
DIL_PAIRS = ((128, 1), (512, 4), (2048, 16))
MLP_HIDDEN = 4 * D_MODEL
ROPE_THETA = 10000.0
RMS_EPS = 1e-6
LN_EPS = 1e-5
NEG_INF = -1e30
LOG2E = math.log2(math.e)

LANES = 128
VMEM_LIMIT_BYTES = 56 * 1024 * 1024

PROJ_ROWS = 256
ATT_TQ = 512
ATT_TK = 512
ROW_TILE = 512
MLP_CHUNK = 1024
AUG_LANE = DIFF_QK_DIM
POS_SPLIT = 64
DIFF_PAIR = 2
DIL_BAND = 1024

_NT = (((1,), (1,)), ((), ()))

F32 = jnp.float32
BF16 = jnp.bfloat16


def _params(**kw):
    return pltpu.CompilerParams(vmem_limit_bytes=VMEM_LIMIT_BYTES, **kw)


def _const_spec(shape):
    nd = len(shape)
    return pl.BlockSpec(shape, lambda *_: (0,) * nd, pipeline_mode=pl.Buffered(1))


def _ada_kernel(c_ref, w_ref, b_ref, o_ref):
    c = c_ref[...]
    cond = c * (1.0 / (1.0 + jnp.exp(-c)))
    o_ref[0, 0] = jnp.dot(cond, w_ref[0], preferred_element_type=F32) + b_ref[0, 0]


def _ada_call(c, w_ada, b_ada):
    n_layers, d, _ = w_ada.shape
    bsz = c.shape[0]
    b4 = b_ada.reshape(n_layers, 6, 1, d)
    out = pl.pallas_call(
        _ada_kernel,
        out_shape=jax.ShapeDtypeStruct((n_layers, 6, bsz, d), F32),
        grid=(n_layers, 6),
        in_specs=[
            pl.BlockSpec((bsz, d), lambda l, j: (0, 0)),
            pl.BlockSpec((1, d, d), lambda l, j: (l, 0, j)),
            pl.BlockSpec((1, 1, 1, d), lambda l, j: (l, j, 0, 0)),
        ],
        out_specs=pl.BlockSpec((1, 1, bsz, d), lambda l, j: (l, j, 0, 0)),
        compiler_params=_params(),
        name="adaln_mod",
    )(c, w_ada, b4)
    return jnp.transpose(out, (0, 2, 1, 3))


_SEG_WIDTHS = (
    ("a_q", 256), ("a_k", 256), ("b_cq", 384), ("b_ckv", 256), ("b_kr", 128),
    ("c_q", 256), ("c_q_sw", 256), ("c_k", 128), ("c_k_sw", 128),
    ("d_q", 256), ("d_k", 256), ("v_a", 512), ("v_c", 256), ("v_d", 512),
)
_SEG = {}
_off = 0
for _name, _w in _SEG_WIDTHS:
    _SEG[_name] = (_off, _off + _w)
    _off += _w
W1_COLS = _off


def _rms_rows(x, gsum):
    sq = x * x
    hi = sq.astype(BF16)
    lo = (sq - hi.astype(F32)).astype(BF16)
    ms = (jnp.dot(hi, gsum, preferred_element_type=F32)
          + jnp.dot(lo, gsum, preferred_element_type=F32))
    return lax.rsqrt(ms + RMS_EPS)


def _proj_kernel(x_ref, mod_ref, w1_ref, pa_ref, wuq_ref, wukv_ref, e_ref, g_ref, pcq_ref, pck_ref,
                 vec_ref, tb_ref, tc_ref, tck_ref, tkr_ref, qaug_ref, kaug_ref,
                 qa_ref, ka_ref, va_ref, qb_ref, kb_ref, vb_ref, qc_ref, kc_ref, vc_ref,
                 qd_ref, kd_ref, vd_ref):
    sh = mod_ref[0, 0:1, :]
    sc = mod_ref[0, 1:2, :]
    h = (x_ref[0] * (1.0 + sc) + sh).astype(BF16)

    def seg(name):
        a, b = _SEG[name]
        return jnp.dot(h, w1_ref[:, a:b], preferred_element_type=F32)

    def vec(row, width):
        return vec_ref[row:row + 1, 0:width]

    a_scale = DIFF_QK_DIM ** -0.5
    qa = jnp.dot((seg("a_q") * a_scale).astype(BF16), pa_ref[...], preferred_element_type=F32)
    qa_ref[0] = qa.astype(BF16) + qaug_ref[...]
    ka = jnp.dot(seg("a_k").astype(BF16), pa_ref[...], preferred_element_type=F32)
    ka_ref[0] = ka.astype(BF16) + kaug_ref[...]
    va_ref[0] = (seg("v_a") + vec(0, 512)).astype(BF16)

    cq = seg("b_cq")
    r = lax.rsqrt(jnp.mean(cq * cq, axis=-1, keepdims=True) + RMS_EPS)
    cqn = (cq * r * vec(1, 384)).astype(BF16)
    q2 = jnp.dot(cqn, wuq_ref[...], preferred_element_type=F32)
    qb_ref[0] = (q2[:, 0:512] * tb_ref[0] + q2[:, 512:1024] * tb_ref[1]).astype(BF16)
    ckv = seg("b_ckv")
    r = lax.rsqrt(jnp.mean(ckv * ckv, axis=-1, keepdims=True) + RMS_EPS)
    ckvn = (ckv * r * vec(2, 256)).astype(BF16)
    kv2 = jnp.dot(ckvn, wukv_ref[...], preferred_element_type=F32)
    kr = (seg("b_kr") * tkr_ref[...]).astype(BF16)
    kb_ref[0] = (kv2[:, 0:512] + jnp.dot(kr, e_ref[...], preferred_element_type=F32)).astype(BF16)
    vb_ref[0] = (kv2[:, 512:1024] + vec(0, 512)).astype(BF16)

    cqx = seg("c_q")
    r = _rms_rows(cqx, g_ref[...])
    qc = cqx * r * vec(3, 256) * tc_ref[0] + seg("c_q_sw") * r * vec(4, 256) * tc_ref[1]
    qc_ref[0] = jnp.dot(qc.astype(BF16), pcq_ref[...], preferred_element_type=F32).astype(BF16)
    ckx = seg("c_k")
    r = _rms_rows(ckx, g_ref[0:128, 0:128])
    kc = ckx * r * vec(5, 128) * tck_ref[0] + seg("c_k_sw") * r * vec(6, 128) * tck_ref[1]
    kc_ref[0] = jnp.dot(kc.astype(BF16), pck_ref[...], preferred_element_type=F32).astype(BF16)
    vc_ref[0] = (seg("v_c") + vec(0, 256)).astype(BF16)

    d_scale = HEAD_DIM ** -0.5 * LOG2E
    qd = jnp.dot((seg("d_q") * d_scale).astype(BF16), pcq_ref[...], preferred_element_type=F32)
    qd_ref[0] = qd.astype(BF16)
    kd_ref[0] = jnp.dot(seg("d_k").astype(BF16), pcq_ref[...], preferred_element_type=F32).astype(BF16)
    vd_ref[0] = (seg("v_d") + vec(0, 512)).astype(BF16)


def _proj_call(x, mod_l, w1, consts, layer_w, tables):
    bsz, seq, d = x.shape
    tm = PROJ_ROWS
    nt = seq // tm
    out_widths = (1024, 1024, 512, 512, 512, 512, 512, 256, 256, 512, 512, 512)
    row_spec = lambda w: pl.BlockSpec((1, tm, w), lambda b, i: (b, i, 0))
    tab3 = lambda w: pl.BlockSpec((2, tm, w), lambda b, i: (0, i, 0))
    tab2 = lambda w: pl.BlockSpec((tm, w), lambda b, i: (i, 0))
    in_specs = [
        row_spec(d),
        pl.BlockSpec((1, 6, d), lambda b, i: (b, 0, 0)),
        _const_spec(w1.shape),
        _const_spec(consts["pa"].shape),
        _const_spec(layer_w["wuq"].shape),
        _const_spec(layer_w["wukv"].shape),
        _const_spec(consts["e"].shape),
        _const_spec(consts["g"].shape),
        _const_spec(consts["pcq"].shape),
        _const_spec(consts["pck"].shape),
        _const_spec(layer_w["vec"].shape),
        tab3(512), tab3(256), tab3(128), tab2(128), tab2(1024), tab2(1024),
    ]
    return pl.pallas_call(
        _proj_kernel,
        out_shape=tuple(jax.ShapeDtypeStruct((bsz, seq, w), BF16) for w in out_widths),
        grid=(bsz, nt),
        in_specs=in_specs,
        out_specs=tuple(row_spec(w) for w in out_widths),
        compiler_params=_params(),
        name="proj_prep",
    )(x, mod_l, w1, consts["pa"], layer_w["wuq"], layer_w["wukv"], consts["e"], consts["g"],
      consts["pcq"], consts["pck"], layer_w["vec"], tables["tb"], tables["tc"], tables["tck"],
      tables["tkr"], tables["qaug"], tables["kaug"])


def _softmax_steps(scores, vs, carry, *, base2):
    ex = jnp.exp2 if base2 else jnp.exp
    stats = []
    for s, (m, _) in zip(scores, carry):
        m_new = jnp.maximum(m, jnp.max(s, axis=-1, keepdims=True))
        stats.append((m_new, ex(s - m_new).astype(BF16), ex(m - m_new)))
    out = []
    for (m_new, p, alpha), v, (_, acc) in zip(stats, vs, carry):
        out.append((m_new, alpha * acc + jnp.dot(p, v, preferred_element_type=F32)))
    return tuple(out)


def _softmax_init(tq):
    return jnp.full((tq, 1), NEG_INF, F32), jnp.zeros((tq, LANES), F32)


def _normalize(acc):
    return acc[:, 0:HEAD_DIM] / acc[:, HEAD_DIM:HEAD_DIM + 1]


def _chunk(ref, off, group):
    return ref[0, pl.ds(off, ATT_TK), LANES * group:LANES * (group + 1)]


def _attn_plain_kernel(q_ref, k_ref, v_ref, o_ref, *, n_heads, q_per_kv):
    tq = q_ref.shape[1]
    nk = k_ref.shape[1] // ATT_TK

    def body(j, carry):
        off = pl.multiple_of(j * ATT_TK, ATT_TK)
        scores = [lax.dot_general(q_ref[0, :, LANES * h:LANES * (h + 1)], _chunk(k_ref, off, h // q_per_kv),
                                  _NT, preferred_element_type=F32) for h in range(n_heads)]
        vs = [_chunk(v_ref, off, h // q_per_kv) for h in range(n_heads)]
        return _softmax_steps(scores, vs, carry, base2=True)

    carry = lax.fori_loop(0, nk, body, tuple(_softmax_init(tq) for _ in range(n_heads)), unroll=True)
    o_ref[0] = jnp.concatenate([_normalize(acc) for _, acc in carry], axis=1).astype(o_ref.dtype)


def _attn_plain_call(q, k, v, *, n_heads, q_per_kv):
    bsz, seq, qw = q.shape
    kw = k.shape[2]
    kern = functools.partial(_attn_plain_kernel, n_heads=n_heads, q_per_kv=q_per_kv)
    return pl.pallas_call(
        kern,
        out_shape=jax.ShapeDtypeStruct((bsz, seq, n_heads * HEAD_DIM), BF16),
        grid=(bsz, seq // ATT_TQ),
        in_specs=[
            pl.BlockSpec((1, ATT_TQ, qw), lambda b, i: (b, i, 0)),
            pl.BlockSpec((1, seq, kw), lambda b, i: (b, 0, 0)),
            pl.BlockSpec((1, seq, kw), lambda b, i: (b, 0, 0)),
        ],
        out_specs=pl.BlockSpec((1, ATT_TQ, n_heads * HEAD_DIM), lambda b, i: (b, i, 0)),
        compiler_params=_params(),
        name="attn_plain",
    )(q, k, v)


def _attn_diff_kernel(q_ref, k_ref, v_ref, lam_ref, g_ref, bias_ref, o_ref):
    tq = q_ref.shape[1]
    nk = k_ref.shape[1] // ATT_TK
    qi = pl.program_id(2)
    n_comp = 2 * DIFF_PAIR
    lane = lax.broadcasted_iota(jnp.int32, (1, LANES), 1)

    def step(off, carry, aug_sign, diagonal):
        aug = jnp.where(lane < AUG_LANE, 1.0, aug_sign).astype(BF16)
        scores = []
        for hc in range(n_comp):
            q = q_ref[0, :, LANES * hc:LANES * (hc + 1)] * aug
            s = lax.dot_general(q, _chunk(k_ref, off, hc), _NT, preferred_element_type=F32)
            if diagonal:
                s = s + bias_ref[hc // 2]
            scores.append(s)
        vs = [_chunk(v_ref, off, hc // 2) for hc in range(n_comp)]
        return _softmax_steps(scores, vs, carry, base2=False)

    carry = step(pl.multiple_of(qi * ATT_TK, ATT_TK), tuple(_softmax_init(tq) for _ in range(n_comp)),
                 0.0, True)
    for j in range(nk - 1):
        jj = j + (j >= qi).astype(jnp.int32)
        sign = jnp.where(jj < qi, 1.0, -1.0)
        carry = step(pl.multiple_of(jj * ATT_TK, ATT_TK), carry, sign, False)

    lf = lam_ref[...]
    lambda_init = lf[4:5, 0:1]
    lam = (jnp.exp(jnp.sum(lf[0:1] * lf[1:2], axis=-1, keepdims=True))
           - jnp.exp(jnp.sum(lf[2:3] * lf[3:4], axis=-1, keepdims=True)) + lambda_init)
    outs = []
    for h in range(DIFF_PAIR):
        y = _normalize(carry[2 * h][1]) - lam * _normalize(carry[2 * h + 1][1])
        r = lax.rsqrt(jnp.mean(y * y, axis=-1, keepdims=True) + RMS_EPS)
        outs.append(y * r * g_ref[...] * (1.0 - lambda_init))
    o_ref[0] = jnp.concatenate(outs, axis=1).astype(o_ref.dtype)


def _attn_diff_call(q, k, v, lam, subln_g, bias):
    bsz, seq, _ = q.shape
    qw = 2 * DIFF_PAIR * LANES
    vw = DIFF_PAIR * LANES
    ow = DIFF_PAIR * HEAD_DIM
    return pl.pallas_call(
        _attn_diff_kernel,
        out_shape=jax.ShapeDtypeStruct((bsz, seq, GROUP_WIDTH), BF16),
        grid=(bsz, N_HEADS // DIFF_PAIR, seq // ATT_TQ),
        in_specs=[
            pl.BlockSpec((1, ATT_TQ, qw), lambda b, p, i: (b, i, p)),
            pl.BlockSpec((1, seq, qw), lambda b, p, i: (b, 0, p)),
            pl.BlockSpec((1, seq, vw), lambda b, p, i: (b, 0, p)),
            _const_spec(lam.shape),
            _const_spec(subln_g.shape),
            pl.BlockSpec((DIFF_PAIR, ATT_TQ, ATT_TK), lambda b, p, i: (p, 0, 0)),
        ],
        out_specs=pl.BlockSpec((1, ATT_TQ, ow), lambda b, p, i: (b, i, p)),
        compiler_params=_params(),
        name="attn_diff",
    )(q, k, v, lam, subln_g, bias)


def _attn_dil_kernel(q_ref, k_ref, v_ref, bias_ref, o_ref):
    tq = q_ref.shape[1]
    nk = k_ref.shape[1] // ATT_TK
    reach = DIL_BAND // ATT_TK
    qi = pl.program_id(1)
    n_off = 2 * reach + 1
    carry = tuple(_softmax_init(tq) for _ in range(N_HEADS))
    for t in [reach] + [t for t in range(n_off) if t != reach]:
        j = qi + (t - reach)
        valid = jnp.logical_and(j >= 0, j < nk)
        off = pl.multiple_of(jnp.clip(j, 0, nk - 1) * ATT_TK, ATT_TK)
        tb = jnp.where(valid, t, n_off)
        scores = [lax.dot_general(q_ref[0, :, LANES * h:LANES * (h + 1)], _chunk(k_ref, off, h), _NT,
                                  preferred_element_type=F32) + bias_ref[h, tb] for h in range(N_HEADS)]
        vs = [_chunk(v_ref, off, h) for h in range(N_HEADS)]
        carry = _softmax_steps(scores, vs, carry, base2=True)
    o_ref[0] = jnp.concatenate([_normalize(acc) for _, acc in carry], axis=1).astype(o_ref.dtype)


def _attn_dil_call(q, k, v, bias):
    bsz, seq, qw = q.shape
    return pl.pallas_call(
        _attn_dil_kernel,
        out_shape=jax.ShapeDtypeStruct((bsz, seq, GROUP_WIDTH), BF16),
        grid=(bsz, seq // ATT_TQ),
        in_specs=[
            pl.BlockSpec((1, ATT_TQ, qw), lambda b, i: (b, i, 0)),
            pl.BlockSpec((1, seq, qw), lambda b, i: (b, 0, 0)),
            pl.BlockSpec((1, seq, qw), lambda b, i: (b, 0, 0)),
            _const_spec(bias.shape),
        ],
        out_specs=pl.BlockSpec((1, ATT_TQ, GROUP_WIDTH), lambda b, i: (b, i, 0)),
        compiler_params=_params(),
        name="attn_dilated",
    )(q, k, v, bias)


def _layer_norm(z, g, b):
    mu = jnp.mean(z, axis=-1, keepdims=True)
    zc = z - mu
    var = jnp.mean(zc * zc, axis=-1, keepdims=True)
    return zc * lax.rsqrt(var + LN_EPS) * g + b


def _outproj_kernel(ya_ref, yb_ref, yc_ref, yd_ref, x_ref, mod_ref, wo_ref, ln_ref, o_ref, *, alpha):
    ycat = jnp.concatenate([ya_ref[0], yb_ref[0], yc_ref[0], yd_ref[0]], axis=1)
    y = jnp.dot(ycat, wo_ref[...], preferred_element_type=F32)
    z = alpha * x_ref[0] + mod_ref[0, 2:3, :] * y
    o_ref[0] = _layer_norm(z, ln_ref[0:1, :], ln_ref[1:2, :])


def _outproj_call(ys, x, mod_l, wo, ln, *, alpha):
    bsz, seq, d = x.shape
    tm = ROW_TILE
    yspec = pl.BlockSpec((1, tm, GROUP_WIDTH), lambda b, i: (b, i, 0))
    xspec = pl.BlockSpec((1, tm, d), lambda b, i: (b, i, 0))
    return pl.pallas_call(
        functools.partial(_outproj_kernel, alpha=alpha),
        out_shape=jax.ShapeDtypeStruct((bsz, seq, d), F32),
        grid=(bsz, seq // tm),
        in_specs=[yspec, yspec, yspec, yspec, xspec,
                  pl.BlockSpec((1, 6, d), lambda b, i: (b, 0, 0)),
                  _const_spec(wo.shape), _const_spec(ln.shape)],
        out_specs=xspec,
        compiler_params=_params(),
        name="out_proj_ln",
    )(*ys, x, mod_l, wo, ln)


def _mlp_kernel(x_ref, mod_ref, wup_ref, wdn_ref, ln_ref, o_ref, *, alpha):
    x = x_ref[0]
    h = (x * (1.0 + mod_ref[0, 4:5, :]) + mod_ref[0, 3:4, :]).astype(BF16)
    u = jnp.zeros(x.shape, F32)
    for c in range(MLP_HIDDEN // MLP_CHUNK):
        a = jnp.dot(h, wup_ref[:, c * MLP_CHUNK:(c + 1) * MLP_CHUNK], preferred_element_type=F32)
        a = jnp.maximum(a, 0.0)
        u = u + jnp.dot((a * a).astype(BF16), wdn_ref[c * MLP_CHUNK:(c + 1) * MLP_CHUNK, :],
                        preferred_element_type=F32)
    z = alpha * x + mod_ref[0, 5:6, :] * u
    o_ref[0] = _layer_norm(z, ln_ref[0:1, :], ln_ref[1:2, :])


def _mlp_call(x, mod_l, wup, wdn, ln, *, alpha):
    bsz, seq, d = x.shape
    tm = ROW_TILE
    xspec = pl.BlockSpec((1, tm, d), lambda b, i: (b, i, 0))
    return pl.pallas_call(
        functools.partial(_mlp_kernel, alpha=alpha),
        out_shape=jax.ShapeDtypeStruct((bsz, seq, d), F32),
        grid=(bsz, seq // tm),
        in_specs=[xspec, pl.BlockSpec((1, 6, d), lambda b, i: (b, 0, 0)),
                  _const_spec(wup.shape), _const_spec(wdn.shape), _const_spec(ln.shape)],
        out_specs=xspec,
        compiler_params=_params(),
        name="mlp_ln",
    )(x, mod_l, wup, wdn, ln)


def _rot_half_source(width, block):
    half = block // 2
    d = np.arange(width)
    inner = d % block
    src = np.where(inner < half, d + half, d - half)
    sign = np.where(inner < half, -1.0, 1.0)
    return src, sign.astype(np.float32)


def _placement(n_in, n_out, pairs):
    m = np.zeros((n_in, n_out), np.float32)
    for r, c in pairs:
        m[r, c] = 1.0
    return jnp.asarray(m, BF16)


def _static_consts():
    pa = _placement(256, 1024, [(32 * hc + d, LANES * hc + d) for hc in range(8) for d in range(32)])
    pcq = _placement(256, 512, [(64 * h + d, LANES * h + d) for h in range(4) for d in range(64)])
    pck = _placement(128, 256, [(64 * g + d, LANES * g + d) for g in range(2) for d in range(64)])
    e = _placement(128, 512, [(s * 32 + d, LANES * h + MLA_NOPE_DIM + d)
                              for h in range(4) for s in range(2) for d in range(32)])
    g = np.zeros((256, 256), np.float32)
    for h in range(4):
        g[64 * h:64 * (h + 1), 64 * h:64 * (h + 1)] = 1.0 / HEAD_DIM
    return dict(pa=pa, pcq=pcq, pck=pck, e=e, g=jnp.asarray(g, BF16))


def _rope_tables(pos, half):
    freqs = ROPE_THETA ** (-jnp.arange(half, dtype=F32) / half)
    ang = pos[:, None] * freqs[None, :]
    cos = jnp.concatenate([jnp.cos(ang), jnp.cos(ang)], axis=1)
    sin = jnp.concatenate([jnp.sin(ang), jnp.sin(ang)], axis=1)
    return cos, sin


def _position_tables(seq, slopes_a, slopes_d):
    pos = jnp.arange(seq, dtype=F32)
    row = jnp.floor(pos / GRID_W)
    col = pos - row * GRID_W
    cos32, sin32 = _rope_tables(pos, MLA_ROPE_DIM // 2)
    zeros32 = jnp.zeros((seq, 32), F32)
    ones64 = jnp.ones((seq, 64), F32)
    zeros64 = jnp.zeros((seq, 64), F32)
    b_scale = MLA_QK_DIM ** -0.5 * LOG2E
    cos_b = jnp.tile(jnp.concatenate([ones64, cos32, zeros32], 1), (1, 4)) * b_scale
    sin_b = jnp.tile(jnp.concatenate([zeros64, sin32, zeros32], 1), (1, 4)) * b_scale
    tkr = jnp.concatenate([cos32, sin32, zeros64], 1)
    cr, sr = _rope_tables(row, HEAD_DIM // 4)
    cc, sc = _rope_tables(col, HEAD_DIM // 4)
    cos_h = jnp.concatenate([cr, cc], 1)
    sin_h = jnp.concatenate([sr, sc], 1)
    c_scale = HEAD_DIM ** -0.5 * LOG2E
    tc = jnp.stack([jnp.tile(cos_h, (1, 4)), jnp.tile(sin_h, (1, 4))]) * c_scale
    tck = jnp.stack([jnp.tile(cos_h, (1, 2)), jnp.tile(sin_h, (1, 2))])
    ipos = np.arange(seq)
    hi = (ipos // POS_SPLIT).astype(np.float32)
    lo = (ipos % POS_SPLIT).astype(np.float32)
    qaug = np.zeros((seq, 8 * LANES), np.float32)
    kaug = np.zeros((seq, 8 * LANES), np.float32)
    for hc in range(8):
        s = float(slopes_a[hc // 2])
        base = LANES * hc + AUG_LANE
        qaug[:, base + 0] = hi
        qaug[:, base + 1] = lo
        qaug[:, base + 2] = 1.0
        qaug[:, base + 3] = 1.0
        kaug[:, base + 0] = -s * POS_SPLIT
        kaug[:, base + 1] = -s
        kaug[:, base + 2] = s * POS_SPLIT * hi
        kaug[:, base + 3] = s * lo
    reach = DIL_BAND // ATT_TK
    a = np.arange(ATT_TQ)[:, None]
    b = np.arange(ATT_TK)[None, :]
    tabs = []
    for t in range(2 * reach + 1):
        o = (t - reach) * ATT_TK + b - a
        cnt = np.zeros(o.shape, np.float32)
        for w, dil in DIL_PAIRS:
            cnt += ((np.abs(o) <= w // 2) & (o % dil == 0)).astype(np.float32)
        tabs.append((np.abs(o).astype(np.float32), cnt))
    dist = np.stack([t[0] for t in tabs])
    cnt = np.stack([t[1] for t in tabs])
    logc = np.where(cnt > 0, np.log(np.maximum(cnt, 1.0)), NEG_INF).astype(np.float32)
    bias_d = np.stack([np.where(cnt > 0, (-float(s) * dist + logc) * LOG2E, NEG_INF) for s in slopes_d])
    bias_d = np.concatenate([bias_d, np.full_like(bias_d[:, :1], NEG_INF)], axis=1)
    bias_a = np.stack([-float(s) * np.abs(a - b).astype(np.float32) for s in slopes_a])
    return dict(tb=jnp.stack([cos_b, sin_b]), tc=tc, tck=tck, tkr=tkr, bias_a=jnp.asarray(bias_a),
                qaug=jnp.asarray(qaug, BF16), kaug=jnp.asarray(kaug, BF16),
                bias_d=jnp.asarray(bias_d.astype(np.float32)))


def _place_cols(w, n_heads, src_stride, src_off, width, dst_width=LANES):
    parts = []
    for h in range(n_heads):
        blk = w[:, h * src_stride + src_off:h * src_stride + src_off + width]
        parts.append(jnp.pad(blk, ((0, 0), (0, dst_width - width))))
    return jnp.concatenate(parts, axis=1)


def _layer_weights(w_in, mla_w_uq, mla_w_ukv, mla_q_norm_g, mla_kv_norm_g, gqa_q_norm_g, gqa_k_norm_g):
    o_aq, o_ak, o_av, o_cq, o_ckv, o_kr = 0, 256, 512, 768, 1152, 1408
    o_gq, o_gk, o_gv, o_dq, o_dk, o_dv = 1440, 1696, 1824, 1952, 2208, 2464
    src32, sign32 = _rot_half_source(32, 32)
    kr = w_in[:, o_kr:o_kr + 32]
    kr_sw = kr[:, src32] * sign32
    srcq, signq = _rot_half_source(256, 32)
    gq = w_in[:, o_gq:o_gq + 256]
    gk = w_in[:, o_gk:o_gk + 128]
    w1 = jnp.concatenate([
        w_in[:, o_aq:o_aq + 256], w_in[:, o_ak:o_ak + 256],
        w_in[:, o_cq:o_cq + 384], w_in[:, o_ckv:o_ckv + 256],
        kr, kr_sw, jnp.zeros((w_in.shape[0], 64), w_in.dtype),
        gq, gq[:, srcq] * signq, gk, gk[:, srcq[:128]] * signq[:128],
        w_in[:, o_dq:o_dq + 256], w_in[:, o_dk:o_dk + 256],
        _place_cols(w_in[:, o_av:o_av + 256], 4, 64, 0, 64),
        _place_cols(w_in[:, o_gv:o_gv + 128], 2, 64, 0, 64),
        _place_cols(w_in[:, o_dv:o_dv + 256], 4, 64, 0, 64),
    ], axis=1).astype(BF16)
    assert w1.shape[1] == W1_COLS
    uq_plain = _place_cols(mla_w_uq, 4, MLA_QK_DIM, 0, MLA_QK_DIM)
    pe = [mla_w_uq[:, h * MLA_QK_DIM + MLA_NOPE_DIM:(h + 1) * MLA_QK_DIM] for h in range(4)]
    uq_sw = jnp.concatenate([
        jnp.pad(p[:, src32] * sign32, ((0, 0), (MLA_NOPE_DIM, LANES - MLA_QK_DIM))) for p in pe], axis=1)
    wuq = jnp.concatenate([uq_plain, uq_sw], axis=1).astype(BF16)
    uk = _place_cols(mla_w_ukv, 4, MLA_NOPE_DIM + MLA_V_DIM, 0, MLA_NOPE_DIM)
    uv = _place_cols(mla_w_ukv, 4, MLA_NOPE_DIM + MLA_V_DIM, MLA_NOPE_DIM, MLA_V_DIM)
    wukv = jnp.concatenate([uk, uv], axis=1).astype(BF16)
    ones = np.zeros((512,), np.float32)
    ones[HEAD_DIM::LANES] = 1.0
    src64, _ = _rot_half_source(64, 32)
    gq_g = jnp.tile(gqa_q_norm_g, 4)
    gq_gs = jnp.tile(gqa_q_norm_g[src64], 4)
    gk_g = jnp.tile(gqa_k_norm_g, 2)
    gk_gs = jnp.tile(gqa_k_norm_g[src64], 2)
    pad = lambda v: jnp.pad(v.astype(F32), (0, 512 - v.shape[0]))
    vec = jnp.stack([jnp.asarray(ones), pad(mla_q_norm_g), pad(mla_kv_norm_g),
                     pad(gq_g), pad(gq_gs), pad(gk_g), pad(gk_gs), jnp.zeros((512,), F32)])
    return w1, dict(wuq=wuq, wukv=wukv, vec=vec)


def kernel(x, c, w_ada, b_ada, w_in, w_o, diff_lambda, diff_subln_g, mla_q_norm_g, mla_w_uq, mla_kv_norm_g, mla_w_ukv, gqa_q_norm_g, gqa_k_norm_g, ln_attn_g, ln_attn_b, w_up, w_down, ln_mlp_g, ln_mlp_b):
    alpha = (2 * DEPTH) ** 0.25
    seq = x.shape[1]
    n_slopes = 2 * N_HEADS
    slopes = [2.0 ** (-8.0 * (n + 1.0) / n_slopes) for n in range(n_slopes)]
    slopes_a, slopes_d = tuple(slopes[0::2]), tuple(slopes[1::2])
    consts = _static_consts()
    tables = _position_tables(seq, slopes_a, slopes_d)
    mod = _ada_call(c, w_ada, b_ada)
    for l in range(DEPTH):
        lambda_init = 0.8 - 0.6 * math.exp(-0.3 * l)
        w1, layer_w = _layer_weights(w_in[l], mla_w_uq[l], mla_w_ukv[l], mla_q_norm_g[l],
                                     mla_kv_norm_g[l], gqa_q_norm_g[l], gqa_k_norm_g[l])
        (qa, ka, va, qb, kb, vb, qc, kc, vc, qd, kd, vd) = _proj_call(
            x, mod[l], w1, consts, layer_w, tables)
        lam_pack = jnp.concatenate([diff_lambda[l].astype(F32),
                                    jnp.full((1, DIFF_QK_DIM), lambda_init, F32)], axis=0)
        ya = _attn_diff_call(qa, ka, va, lam_pack, diff_subln_g[l].reshape(1, HEAD_DIM), tables["bias_a"])
        yb = _attn_plain_call(qb, kb, vb, n_heads=N_HEADS, q_per_kv=1)
        yc = _attn_plain_call(qc, kc, vc, n_heads=N_HEADS, q_per_kv=N_HEADS // GQA_KV_HEADS)
        yd = _attn_dil_call(qd, kd, vd, tables["bias_d"])
        x = _outproj_call((ya, yb, yc, yd), x, mod[l], w_o[l].astype(BF16),
                          jnp.stack([ln_attn_g[l], ln_attn_b[l]]), alpha=alpha)
        x = _mlp_call(x, mod[l], w_up[l].astype(BF16), w_down[l].astype(BF16),
                      jnp.stack([ln_mlp_g[l], ln_mlp_b[l]]), alpha=alpha)
    return x
```

```python
import functools
import math

import jax
import jax.numpy as jnp
import ml_dtypes
import numpy as np
from jax import lax
from jax.experimental import pallas as pl
from jax.experimental.pallas import tpu as pltpu

D_MODEL = 1024
DEPTH = 2
HEAD_DIM = 64
GROUP_WIDTH = D_MODEL // 4
N_HEADS = GROUP_WIDTH // HEAD_DIM
DIFF_QK_DIM = HEAD_DIM // 2
MLA_NOPE_DIM = HEAD_DIM
MLA_ROPE_DIM = HEAD_DIM // 2
MLA_V_DIM = HEAD_DIM
MLA_QK_DIM = MLA_NOPE_DIM + MLA_ROPE_DIM
MLA_Q_RANK = N_HEADS * MLA_QK_DIM
MLA_KV_RANK = 4 * MLA_V_DIM
GQA_KV_HEADS = N_HEADS // 2
GRID_W = 64
DIL_PAIRS = ((128, 1), (512, 4), (2048, 16))
MLP_HIDDEN = 4 * D_MODEL
ROPE_THETA = 10000.0
RMS_EPS = 1e-6
LN_EPS = 1e-5
NEG_INF = -1e30
LOG2E = math.log2(math.e)

LANES = 128
VMEM_LIMIT_BYTES = 56 * 1024 * 1024

PROJ_ROWS = 512
ATT_TQ = 512
ATT_TK = 512
ROW_TILE = 512
MLP_CHUNK = 1024
AUG_LANE = DIFF_QK_DIM
LOG2E_TERMS = 5
POS_SPLIT = 64
DIFF_PAIR = 2
DIL_BAND = 1024

_NT = (((1,), (1,)), ((), ()))

F32 = jnp.float32
BF16 = jnp.bfloat16


def _params(**kw):
    return pltpu.CompilerParams(vmem_limit_bytes=VMEM_LIMIT_BYTES, **kw)


def _const_spec(shape):
    nd = len(shape)
    return pl.BlockSpec(shape, lambda *_: (0,) * nd, pipeline_mode=pl.Buffered(1))


def _ada_kernel(c_ref, w_ref, b_ref, o_ref):
    c = c_ref[...]
    cond = c * (1.0 / (1.0 + jnp.exp(-c)))
    o_ref[0, 0] = jnp.dot(cond, w_ref[0], preferred_element_type=F32) + b_ref[0, 0]


def _ada_call(c, w_ada, b_ada):
    n_layers, d, _ = w_ada.shape
    bsz = c.shape[0]
    b4 = b_ada.reshape(n_layers, 6, 1, d)
    out = pl.pallas_call(
        _ada_kernel,
        out_shape=jax.ShapeDtypeStruct((n_layers, 6, bsz, d), F32),
        grid=(n_layers, 6),
        in_specs=[
            pl.BlockSpec((bsz, d), lambda l, j: (0, 0)),
            pl.BlockSpec((1, d, d), lambda l, j: (l, 0, j)),
            pl.BlockSpec((1, 1, 1, d), lambda l, j: (l, j, 0, 0)),
        ],
        out_specs=pl.BlockSpec((1, 1, bsz, d), lambda l, j: (l, j, 0, 0)),
        compiler_params=_params(),
        name="adaln_mod",
    )(c, w_ada, b4)
    return jnp.transpose(out, (0, 2, 1, 3))


_SEG_WIDTHS = (
    ("a_q", 256), ("a_k", 256), ("b_cq", 384), ("b_ckv", 256), ("b_kr", 128),
    ("c_q", 256), ("c_q_sw", 256), ("c_k", 128), ("c_k_sw", 128),
    ("d_q", 256), ("d_k", 256), ("v_a", 512), ("v_c", 256), ("v_d", 512),
)
_SEG = {}
_off = 0
for _name, _w in _SEG_WIDTHS:
    _SEG[_name] = (_off, _off + _w)
    _off += _w
W1_COLS = _off


def _rms_rows(x, gsum):
    sq = x * x
    hi = sq.astype(BF16)
    lo = (sq - hi.astype(F32)).astype(BF16)
    ms = (jnp.dot(hi, gsum, preferred_element_type=F32)
          + jnp.dot(lo, gsum, preferred_element_type=F32))
    return lax.rsqrt(ms + RMS_EPS)


def _proj_kernel(x_ref, mod_ref, w1_ref, pa_ref, wuq_ref, wukv_ref, e_ref, g_ref, pcq_ref, pck_ref,
                 vec_ref, tb_ref, tc_ref, tck_ref, tkr_ref, qaug_ref, kaug_ref,
                 qa_ref, ka_ref, va_ref, qb_ref, kb_ref, vb_ref, qc_ref, kc_ref, vc_ref,
                 qd_ref, kd_ref, vd_ref):
    sh = mod_ref[0, 0:1, :]
    sc = mod_ref[0, 1:2, :]
    h = (x_ref[0] * (1.0 + sc) + sh).astype(BF16)

    def seg(name):
        a, b = _SEG[name]
        return jnp.dot(h, w1_ref[:, a:b], preferred_element_type=F32)

    def vec(row, width):
        return vec_ref[row:row + 1, 0:width]

    a_scale = DIFF_QK_DIM ** -0.5 * LOG2E
    qa = jnp.dot((seg("a_q") * a_scale).astype(BF16), pa_ref[...], preferred_element_type=F32)
    qa_ref[0] = qa.astype(BF16) + qaug_ref[...]
    ka = jnp.dot(seg("a_k").astype(BF16), pa_ref[...], preferred_element_type=F32)
    ka_ref[0] = ka.astype(BF16) + kaug_ref[...]
    va_ref[0] = (seg("v_a") + vec(0, 512)).astype(BF16)

    cq = seg("b_cq")
    r = lax.rsqrt(jnp.mean(cq * cq, axis=-1, keepdims=True) + RMS_EPS)
    cqn = (cq * r * vec(1, 384)).astype(BF16)
    q2 = jnp.dot(cqn, wuq_ref[...], preferred_element_type=F32)
    qb_ref[0] = (q2[:, 0:512] * tb_ref[0] + q2[:, 512:1024] * tb_ref[1]).astype(BF16)
    ckv = seg("b_ckv")
    r = lax.rsqrt(jnp.mean(ckv * ckv, axis=-1, keepdims=True) + RMS_EPS)
    ckvn = (ckv * r * vec(2, 256)).astype(BF16)
    kv2 = jnp.dot(ckvn, wukv_ref[...], preferred_element_type=F32)
    kr = (seg("b_kr") * tkr_ref[...]).astype(BF16)
    kb_ref[0] = (kv2[:, 0:512] + jnp.dot(kr, e_ref[...], preferred_element_type=F32)).astype(BF16)
    vb_ref[0] = (kv2[:, 512:1024] + vec(0, 512)).astype(BF16)

    cqx = seg("c_q")
    r = _rms_rows(cqx, g_ref[...])
    qc = cqx * r * vec(3, 256) * tc_ref[0] + seg("c_q_sw") * r * vec(4, 256) * tc_ref[1]
    qc_ref[0] = jnp.dot(qc.astype(BF16), pcq_ref[...], preferred_element_type=F32).astype(BF16)
    ckx = seg("c_k")
    r = _rms_rows(ckx, g_ref[0:128, 0:128])
    kc = ckx * r * vec(5, 128) * tck_ref[0] + seg("c_k_sw") * r * vec(6, 128) * tck_ref[1]
    kc_ref[0] = jnp.dot(kc.astype(BF16), pck_ref[...], preferred_element_type=F32).astype(BF16)
    vc_ref[0] = (seg("v_c") + vec(0, 256)).astype(BF16)

    d_scale = HEAD_DIM ** -0.5 * LOG2E
    qd = jnp.dot((seg("d_q") * d_scale).astype(BF16), pcq_ref[...], preferred_element_type=F32)
    qd_ref[0] = qd.astype(BF16)
    kd_ref[0] = jnp.dot(seg("d_k").astype(BF16), pcq_ref[...], preferred_element_type=F32).astype(BF16)
    vd_ref[0] = (seg("v_d") + vec(0, 512)).astype(BF16)


def _proj_call(x, mod_l, w1, consts, layer_w, tables):
    bsz, seq, d = x.shape
    tm = PROJ_ROWS
    nt = seq // tm
    out_widths = (1024, 1024, 512, 512, 512, 512, 512, 256, 256, 512, 512, 512)
    row_spec = lambda w: pl.BlockSpec((1, tm, w), lambda b, i: (b, i, 0))
    tab3 = lambda w: pl.BlockSpec((2, tm, w), lambda b, i: (0, i, 0))
    tab2 = lambda w: pl.BlockSpec((tm, w), lambda b, i: (i, 0))
    in_specs = [
        row_spec(d),
        pl.BlockSpec((1, 6, d), lambda b, i: (b, 0, 0)),
        _const_spec(w1.shape),
        _const_spec(consts["pa"].shape),
        _const_spec(layer_w["wuq"].shape),
        _const_spec(layer_w["wukv"].shape),
        _const_spec(consts["e"].shape),
        _const_spec(consts["g"].shape),
        _const_spec(consts["pcq"].shape),
        _const_spec(consts["pck"].shape),
        _const_spec(layer_w["vec"].shape),
        tab3(512), tab3(256), tab3(128), tab2(128), tab2(1024), tab2(1024),
    ]
    return pl.pallas_call(
        _proj_kernel,
        out_shape=tuple(jax.ShapeDtypeStruct((bsz, seq, w), BF16) for w in out_widths),
        grid=(bsz, nt),
        in_specs=in_specs,
        out_specs=tuple(row_spec(w) for w in out_widths),
        compiler_params=_params(),
        name="proj_prep",
    )(x, mod_l, w1, consts["pa"], layer_w["wuq"], layer_w["wukv"], consts["e"], consts["g"],
      consts["pcq"], consts["pck"], layer_w["vec"], tables["tb"], tables["tc"], tables["tck"],
      tables["tkr"], tables["qaug"], tables["kaug"])


def _softmax_steps(scores, vs, carry, *, base2):
    ex = jnp.exp2 if base2 else jnp.exp
    stats = []
    for s, (m, _) in zip(scores, carry):
        m_new = jnp.maximum(m, jnp.max(s, axis=-1, keepdims=True))
        stats.append((m_new, ex(s - m_new).astype(BF16), ex(m - m_new)))
    out = []
    for (m_new, p, alpha), v, (_, acc) in zip(stats, vs, carry):
        out.append((m_new, alpha * acc + jnp.dot(p, v, preferred_element_type=F32)))
    return tuple(out)


def _softmax_init(tq):
    return jnp.full((tq, 1), NEG_INF, F32), jnp.zeros((tq, LANES), F32)


def _normalize(acc):
    return acc[:, 0:HEAD_DIM] / acc[:, HEAD_DIM:HEAD_DIM + 1]


def _chunk(ref, off, group):
    return ref[0, pl.ds(off, ATT_TK), LANES * group:LANES * (group + 1)]


def _attn_plain_kernel(q_ref, k_ref, v_ref, o_ref, *, n_heads, q_per_kv):
    tq = q_ref.shape[1]
    nk = k_ref.shape[1] // ATT_TK
    carry = tuple(_softmax_init(tq) for _ in range(n_heads))
    for j in range(nk):
        off = j * ATT_TK
        scores = [lax.dot_general(q_ref[0, :, LANES * h:LANES * (h + 1)], _chunk(k_ref, off, h // q_per_kv),
                                  _NT, preferred_element_type=F32) for h in range(n_heads)]
        vs = [_chunk(v_ref, off, h // q_per_kv) for h in range(n_heads)]
        carry = _softmax_steps(scores, vs, carry, base2=True)
    o_ref[0] = jnp.concatenate([_normalize(acc) for _, acc in carry], axis=1).astype(o_ref.dtype)


def _attn_plain_call(q, k, v, *, n_heads, q_per_kv):
    bsz, seq, qw = q.shape
    kw = k.shape[2]
    kern = functools.partial(_attn_plain_kernel, n_heads=n_heads, q_per_kv=q_per_kv)
    return pl.pallas_call(
        kern,
        out_shape=jax.ShapeDtypeStruct((bsz, seq, n_heads * HEAD_DIM), BF16),
        grid=(bsz, seq // ATT_TQ),
        in_specs=[
            pl.BlockSpec((1, ATT_TQ, qw), lambda b, i: (b, i, 0)),
            pl.BlockSpec((1, seq, kw), lambda b, i: (b, 0, 0)),
            pl.BlockSpec((1, seq, kw), lambda b, i: (b, 0, 0)),
        ],
        out_specs=pl.BlockSpec((1, ATT_TQ, n_heads * HEAD_DIM), lambda b, i: (b, i, 0)),
        compiler_params=_params(),
        name="attn_plain",
    )(q, k, v)


def _attn_diff_kernel(q_ref, k_ref, v_ref, lam_ref, g_ref, bias_ref, o_ref):
    tq = q_ref.shape[1]
    nk = k_ref.shape[1] // ATT_TK
    qi = pl.program_id(2)
    n_comp = 2 * DIFF_PAIR
    lane = lax.broadcasted_iota(jnp.int32, (1, LANES), 1)

    def step(off, carry, aug_sign, diagonal):
        aug = jnp.where(lane < AUG_LANE, 1.0, aug_sign).astype(BF16)
        scores = []
        for hc in range(n_comp):
            q = q_ref[0, :, LANES * hc:LANES * (hc + 1)] * aug
            s = lax.dot_general(q, _chunk(k_ref, off, hc), _NT, preferred_element_type=F32)
            if diagonal:
                s = s + bias_ref[hc // 2]
            scores.append(s)
        vs = [_chunk(v_ref, off, hc // 2) for hc in range(n_comp)]
        return _softmax_steps(scores, vs, carry, base2=True)

    carry = step(pl.multiple_of(qi * ATT_TK, ATT_TK), tuple(_softmax_init(tq) for _ in range(n_comp)),
                 0.0, True)
    for j in range(nk - 1):
        jj = j + (j >= qi).astype(jnp.int32)
        sign = jnp.where(jj < qi, 1.0, -1.0)
        carry = step(pl.multiple_of(jj * ATT_TK, ATT_TK), carry, sign, False)

    lf = lam_ref[...]
    lambda_init = lf[4:5, 0:1]
    lam = (jnp.exp(jnp.sum(lf[0:1] * lf[1:2], axis=-1, keepdims=True))
           - jnp.exp(jnp.sum(lf[2:3] * lf[3:4], axis=-1, keepdims=True)) + lambda_init)
    outs = []
    for h in range(DIFF_PAIR):
        y = _normalize(carry[2 * h][1]) - lam * _normalize(carry[2 * h + 1][1])
        r = lax.rsqrt(jnp.mean(y * y, axis=-1, keepdims=True) + RMS_EPS)
        outs.append(y * r * g_ref[...] * (1.0 - lambda_init))
    o_ref[0] = jnp.concatenate(outs, axis=1).astype(o_ref.dtype)


def _attn_diff_call(q, k, v, lam, subln_g, bias):
    bsz, seq, _ = q.shape
    qw = 2 * DIFF_PAIR * LANES
    vw = DIFF_PAIR * LANES
    ow = DIFF_PAIR * HEAD_DIM
    return pl.pallas_call(
        _attn_diff_kernel,
        out_shape=jax.ShapeDtypeStruct((bsz, seq, GROUP_WIDTH), BF16),
        grid=(bsz, N_HEADS // DIFF_PAIR, seq // ATT_TQ),
        in_specs=[
            pl.BlockSpec((1, ATT_TQ, qw), lambda b, p, i: (b, i, p)),
            pl.BlockSpec((1, seq, qw), lambda b, p, i: (b, 0, p)),
            pl.BlockSpec((1, seq, vw), lambda b, p, i: (b, 0, p)),
            _const_spec(lam.shape),
            _const_spec(subln_g.shape),
            pl.BlockSpec((DIFF_PAIR, ATT_TQ, ATT_TK), lambda b, p, i: (p, 0, 0)),
        ],
        out_specs=pl.BlockSpec((1, ATT_TQ, ow), lambda b, p, i: (b, i, p)),
        compiler_params=_params(),
        name="attn_diff",
    )(q, k, v, lam, subln_g, bias)


def _attn_dil_kernel(q_ref, k_ref, v_ref, bias_ref, o_ref):
    tq = q_ref.shape[1]
    nk = k_ref.shape[1] // ATT_TK
    reach = DIL_BAND // ATT_TK
    qi = pl.program_id(1)
    n_off = 2 * reach + 1
    carry = tuple(_softmax_init(tq) for _ in range(N_HEADS))
    for t in [reach] + [t for t in range(n_off) if t != reach]:
        j = qi + (t - reach)
        valid = jnp.logical_and(j >= 0, j < nk)
        off = pl.multiple_of(jnp.clip(j, 0, nk - 1) * ATT_TK, ATT_TK)
        tb = jnp.where(valid, t, n_off)
        scores = [lax.dot_general(q_ref[0, :, LANES * h:LANES * (h + 1)], _chunk(k_ref, off, h), _NT,
                                  preferred_element_type=F32) + bias_ref[h, tb] for h in range(N_HEADS)]
        vs = [_chunk(v_ref, off, h) for h in range(N_HEADS)]
        carry = _softmax_steps(scores, vs, carry, base2=True)
    o_ref[0] = jnp.concatenate([_normalize(acc) for _, acc in carry], axis=1).astype(o_ref.dtype)


def _attn_dil_call(q, k, v, bias):
    bsz, seq, qw = q.shape
    return pl.pallas_call(
        _attn_dil_kernel,
        out_shape=jax.ShapeDtypeStruct((bsz, seq, GROUP_WIDTH), BF16),
        grid=(bsz, seq // ATT_TQ),
        in_specs=[
            pl.BlockSpec((1, ATT_TQ, qw), lambda b, i: (b, i, 0)),
            pl.BlockSpec((1, seq, qw), lambda b, i: (b, 0, 0)),
            pl.BlockSpec((1, seq, qw), lambda b, i: (b, 0, 0)),
            _const_spec(bias.shape),
        ],
        out_specs=pl.BlockSpec((1, ATT_TQ, GROUP_WIDTH), lambda b, i: (b, i, 0)),
        compiler_params=_params(),
        name="attn_dilated",
    )(q, k, v, bias)


def _layer_norm(z, g, b):
    mu = jnp.mean(z, axis=-1, keepdims=True)
    zc = z - mu
    var = jnp.mean(zc * zc, axis=-1, keepdims=True)
    return zc * lax.rsqrt(var + LN_EPS) * g + b


def _outproj_kernel(ya_ref, yb_ref, yc_ref, yd_ref, x_ref, mod_ref, wo_ref, ln_ref, o_ref, *, alpha):
    ycat = jnp.concatenate([ya_ref[0], yb_ref[0], yc_ref[0], yd_ref[0]], axis=1)
    y = jnp.dot(ycat, wo_ref[...], preferred_element_type=F32)
    z = alpha * x_ref[0] + mod_ref[0, 2:3, :] * y
    o_ref[0] = _layer_norm(z, ln_ref[0:1, :], ln_ref[1:2, :])


def _outproj_call(ys, x, mod_l, wo, ln, *, alpha):
    bsz, seq, d = x.shape
    tm = ROW_TILE
    yspec = pl.BlockSpec((1, tm, GROUP_WIDTH), lambda b, i: (b, i, 0))
    xspec = pl.BlockSpec((1, tm, d), lambda b, i: (b, i, 0))
    return pl.pallas_call(
        functools.partial(_outproj_kernel, alpha=alpha),
        out_shape=jax.ShapeDtypeStruct((bsz, seq, d), F32),
        grid=(bsz, seq // tm),
        in_specs=[yspec, yspec, yspec, yspec, xspec,
                  pl.BlockSpec((1, 6, d), lambda b, i: (b, 0, 0)),
                  _const_spec(wo.shape), _const_spec(ln.shape)],
        out_specs=xspec,
        compiler_params=_params(),
        name="out_proj_ln",
    )(*ys, x, mod_l, wo, ln)


def _mlp_kernel(x_ref, mod_ref, wup_ref, wdn_ref, ln_ref, o_ref, *, alpha):
    x = x_ref[0]
    h = (x * (1.0 + mod_ref[0, 4:5, :]) + mod_ref[0, 3:4, :]).astype(BF16)
    u = jnp.zeros(x.shape, F32)
    for c in range(MLP_HIDDEN // MLP_CHUNK):
        a = jnp.dot(h, wup_ref[:, c * MLP_CHUNK:(c + 1) * MLP_CHUNK], preferred_element_type=F32)
        a = jnp.maximum(a, 0.0)
        u = u + jnp.dot((a * a).astype(BF16), wdn_ref[c * MLP_CHUNK:(c + 1) * MLP_CHUNK, :],
                        preferred_element_type=F32)
    z = alpha * x + mod_ref[0, 5:6, :] * u
    o_ref[0] = _layer_norm(z, ln_ref[0:1, :], ln_ref[1:2, :])


def _mlp_call(x, mod_l, wup, wdn, ln, *, alpha):
    bsz, seq, d = x.shape
    tm = ROW_TILE
    xspec = pl.BlockSpec((1, tm, d), lambda b, i: (b, i, 0))
    return pl.pallas_call(
        functools.partial(_mlp_kernel, alpha=alpha),
        out_shape=jax.ShapeDtypeStruct((bsz, seq, d), F32),
        grid=(bsz, seq // tm),
        in_specs=[xspec, pl.BlockSpec((1, 6, d), lambda b, i: (b, 0, 0)),
                  _const_spec(wup.shape), _const_spec(wdn.shape), _const_spec(ln.shape)],
        out_specs=xspec,
        compiler_params=_params(),
        name="mlp_ln",
    )(x, mod_l, wup, wdn, ln)


def _rot_half_source(width, block):
    half = block // 2
    d = np.arange(width)
    inner = d % block
    src = np.where(inner < half, d + half, d - half)
    sign = np.where(inner < half, -1.0, 1.0)
    return src, sign.astype(np.float32)


def _placement(n_in, n_out, pairs):
    m = np.zeros((n_in, n_out), np.float32)
    for r, c in pairs:
        m[r, c] = 1.0
    return jnp.asarray(m, BF16)


def _static_consts():
    pa = _placement(256, 1024, [(32 * hc + d, LANES * hc + d) for hc in range(8) for d in range(32)])
    pcq = _placement(256, 512, [(64 * h + d, LANES * h + d) for h in range(4) for d in range(64)])
    pck = _placement(128, 256, [(64 * g + d, LANES * g + d) for g in range(2) for d in range(64)])
    e = _placement(128, 512, [(s * 32 + d, LANES * h + MLA_NOPE_DIM + d)
                              for h in range(4) for s in range(2) for d in range(32)])
    g = np.zeros((256, 256), np.float32)
    for h in range(4):
        g[64 * h:64 * (h + 1), 64 * h:64 * (h + 1)] = 1.0 / HEAD_DIM
    return dict(pa=pa, pcq=pcq, pck=pck, e=e, g=jnp.asarray(g, BF16))


def _rope_tables(pos, half):
    freqs = ROPE_THETA ** (-jnp.arange(half, dtype=F32) / half)
    ang = pos[:, None] * freqs[None, :]
    cos = jnp.concatenate([jnp.cos(ang), jnp.cos(ang)], axis=1)
    sin = jnp.concatenate([jnp.sin(ang), jnp.sin(ang)], axis=1)
    return cos, sin


def _position_tables(seq, slopes_a, slopes_d):
    pos = jnp.arange(seq, dtype=F32)
    row = jnp.floor(pos / GRID_W)
    col = pos - row * GRID_W
    cos32, sin32 = _rope_tables(pos, MLA_ROPE_DIM // 2)
    zeros32 = jnp.zeros((seq, 32), F32)
    ones64 = jnp.ones((seq, 64), F32)
    zeros64 = jnp.zeros((seq, 64), F32)
    b_scale = MLA_QK_DIM ** -0.5 * LOG2E
    cos_b = jnp.tile(jnp.concatenate([ones64, cos32, zeros32], 1), (1, 4)) * b_scale
    sin_b = jnp.tile(jnp.concatenate([zeros64, sin32, zeros32], 1), (1, 4)) * b_scale
    tkr = jnp.concatenate([cos32, sin32, zeros64], 1)
    cr, sr = _rope_tables(row, HEAD_DIM // 4)
    cc, sc = _rope_tables(col, HEAD_DIM // 4)
    cos_h = jnp.concatenate([cr, cc], 1)
    sin_h = jnp.concatenate([sr, sc], 1)
    c_scale = HEAD_DIM ** -0.5 * LOG2E
    tc = jnp.stack([jnp.tile(cos_h, (1, 4)), jnp.tile(sin_h, (1, 4))]) * c_scale
    tck = jnp.stack([jnp.tile(cos_h, (1, 2)), jnp.tile(sin_h, (1, 2))])
    ipos = np.arange(seq)
    hi = (ipos // POS_SPLIT).astype(np.float32)
    lo = (ipos % POS_SPLIT).astype(np.float32)
    qaug = np.zeros((seq, 8 * LANES), np.float32)
    kaug = np.zeros((seq, 8 * LANES), np.float32)
    terms, rest = [], LOG2E
    for _ in range(LOG2E_TERMS):
        t = float(np.float32(rest).astype(ml_dtypes.bfloat16))
        terms.append(t)
        rest -= t
    for hc in range(8):
        s = float(slopes_a[hc // 2])
        for n, t in enumerate(terms):
            base = LANES * hc + AUG_LANE + 4 * n
            qaug[:, base + 0] = hi
            qaug[:, base + 1] = lo
            qaug[:, base + 2] = s * t * POS_SPLIT
            qaug[:, base + 3] = s * t
            kaug[:, base + 0] = -s * t * POS_SPLIT
            kaug[:, base + 1] = -s * t
            kaug[:, base + 2] = hi
            kaug[:, base + 3] = lo
    reach = DIL_BAND // ATT_TK
    a = np.arange(ATT_TQ)[:, None]
    b = np.arange(ATT_TK)[None, :]
    tabs = []
    for t in range(2 * reach + 1):
        o = (t - reach) * ATT_TK + b - a
        cnt = np.zeros(o.shape, np.float32)
        for w, dil in DIL_PAIRS:
            cnt += ((np.abs(o) <= w // 2) & (o % dil == 0)).astype(np.float32)
        tabs.append((np.abs(o).astype(np.float32), cnt))
    dist = np.stack([t[0] for t in tabs])
    cnt = np.stack([t[1] for t in tabs])
    logc = np.where(cnt > 0, np.log(np.maximum(cnt, 1.0)), NEG_INF).astype(np.float32)
    bias_d = np.stack([np.where(cnt > 0, (-float(s) * dist + logc) * LOG2E, NEG_INF) for s in slopes_d])
    bias_d = np.concatenate([bias_d, np.full_like(bias_d[:, :1], NEG_INF)], axis=1)
    bias_a = np.stack([(-float(s) * LOG2E * np.abs(a - b)).astype(np.float32) for s in slopes_a])
    return dict(tb=jnp.stack([cos_b, sin_b]), tc=tc, tck=tck, tkr=tkr, bias_a=jnp.asarray(bias_a),
                qaug=jnp.asarray(qaug, BF16), kaug=jnp.asarray(kaug, BF16),
                bias_d=jnp.asarray(bias_d.astype(np.float32)))


def _place_cols(w, n_heads, src_stride, src_off, width, dst_width=LANES):
    parts = []
    for h in range(n_heads):
        blk = w[:, h * src_stride + src_off:h * src_stride + src_off + width]
        parts.append(jnp.pad(blk, ((0, 0), (0, dst_width - width))))
    return jnp.concatenate(parts, axis=1)


def _layer_weights(w_in, mla_w_uq, mla_w_ukv, mla_q_norm_g, mla_kv_norm_g, gqa_q_norm_g, gqa_k_norm_g):
    o_aq, o_ak, o_av, o_cq, o_ckv, o_kr = 0, 256, 512, 768, 1152, 1408
    o_gq, o_gk, o_gv, o_dq, o_dk, o_dv = 1440, 1696, 1824, 1952, 2208, 2464
    src32, sign32 = _rot_half_source(32, 32)
    kr = w_in[:, o_kr:o_kr + 32]
    kr_sw = kr[:, src32] * sign32
    srcq, signq = _rot_half_source(256, 32)
    gq = w_in[:, o_gq:o_gq + 256]
    gk = w_in[:, o_gk:o_gk + 128]
    w1 = jnp.concatenate([
        w_in[:, o_aq:o_aq + 256], w_in[:, o_ak:o_ak + 256],
        w_in[:, o_cq:o_cq + 384], w_in[:, o_ckv:o_ckv + 256],
        kr, kr_sw, jnp.zeros((w_in.shape[0], 64), w_in.dtype),
        gq, gq[:, srcq] * signq, gk, gk[:, srcq[:128]] * signq[:128],
        w_in[:, o_dq:o_dq + 256], w_in[:, o_dk:o_dk + 256],
        _place_cols(w_in[:, o_av:o_av + 256], 4, 64, 0, 64),
        _place_cols(w_in[:, o_gv:o_gv + 128], 2, 64, 0, 64),
        _place_cols(w_in[:, o_dv:o_dv + 256], 4, 64, 0, 64),
    ], axis=1).astype(BF16)
    assert w1.shape[1] == W1_COLS
    uq_plain = _place_cols(mla_w_uq, 4, MLA_QK_DIM, 0, MLA_QK_DIM)
    pe = [mla_w_uq[:, h * MLA_QK_DIM + MLA_NOPE_DIM:(h + 1) * MLA_QK_DIM] for h in range(4)]
    uq_sw = jnp.concatenate([
        jnp.pad(p[:, src32] * sign32, ((0, 0), (MLA_NOPE_DIM, LANES - MLA_QK_DIM))) for p in pe], axis=1)
    wuq = jnp.concatenate([uq_plain, uq_sw], axis=1).astype(BF16)
    uk = _place_cols(mla_w_ukv, 4, MLA_NOPE_DIM + MLA_V_DIM, 0, MLA_NOPE_DIM)
    uv = _place_cols(mla_w_ukv, 4, MLA_NOPE_DIM + MLA_V_DIM, MLA_NOPE_DIM, MLA_V_DIM)
    wukv = jnp.concatenate([uk, uv], axis=1).astype(BF16)
    ones = np.zeros((512,), np.float32)
    ones[HEAD_DIM::LANES] = 1.0
    src64, _ = _rot_half_source(64, 32)
    gq_g = jnp.tile(gqa_q_norm_g, 4)
    gq_gs = jnp.tile(gqa_q_norm_g[src64], 4)
    gk_g = jnp.tile(gqa_k_norm_g, 2)
    gk_gs = jnp.tile(gqa_k_norm_g[src64], 2)
    pad = lambda v: jnp.pad(v.astype(F32), (0, 512 - v.shape[0]))
    vec = jnp.stack([jnp.asarray(ones), pad(mla_q_norm_g), pad(mla_kv_norm_g),
                     pad(gq_g), pad(gq_gs), pad(gk_g), pad(gk_gs), jnp.zeros((512,), F32)])
    return w1, dict(wuq=wuq, wukv=wukv, vec=vec)


def kernel(x, c, w_ada, b_ada, w_in, w_o, diff_lambda, diff_subln_g, mla_q_norm_g, mla_w_uq, mla_kv_norm_g, mla_w_ukv, gqa_q_norm_g, gqa_k_norm_g, ln_attn_g, ln_attn_b, w_up, w_down, ln_mlp_g, ln_mlp_b):
    alpha = (2 * DEPTH) ** 0.25
    seq = x.shape[1]
    n_slopes = 2 * N_HEADS
    slopes = [2.0 ** (-8.0 * (n + 1.0) / n_slopes) for n in range(n_slopes)]
    slopes_a, slopes_d = tuple(slopes[0::2]), tuple(slopes[1::2])
    consts = _static_consts()
    tables = _position_tables(seq, slopes_a, slopes_d)
    mod = _ada_call(c, w_ada, b_ada)
    for l in range(DEPTH):
        lambda_init = 0.8 - 0.6 * math.exp(-0.3 * l)
        w1, layer_w = _layer_weights(w_in[l], mla_w_uq[l], mla_w_ukv[l], mla_q_norm_g[l],
                                     mla_kv_norm_g[l], gqa_q_norm_g[l], gqa_k_norm_g[l])
        (qa, ka, va, qb, kb, vb, qc, kc, vc, qd, kd, vd) = _proj_call(
            x, mod[l], w1, consts, layer_w, tables)
        lam_pack = jnp.concatenate([diff_lambda[l].astype(F32),
                                    jnp.full((1, DIFF_QK_DIM), lambda_init, F32)], axis=0)
        ya = _attn_diff_call(qa, ka, va, lam_pack, diff_subln_g[l].reshape(1, HEAD_DIM), tables["bias_a"])
        yb = _attn_plain_call(qb, kb, vb, n_heads=N_HEADS, q_per_kv=1)
        yc = _attn_plain_call(qc, kc, vc, n_heads=N_HEADS, q_per_kv=N_HEADS // GQA_KV_HEADS)
        yd = _attn_dil_call(qd, kd, vd, tables["bias_d"])
        x = _outproj_call((ya, yb, yc, yd), x, mod[l], w_o[l].astype(BF16),
                          jnp.stack([ln_attn_g[l], ln_attn_b[l]]), alpha=alpha)
        x = _mlp_call(x, mod[l], w_up[l].astype(BF16), w_down[l].astype(BF16),
                      jnp.stack([ln_mlp_g[l], ln_mlp_b[l]]), alpha=alpha)
    return x
```

```python
import functools
import math

import jax
import jax.numpy as jnp
import ml_dtypes
import numpy as np
from jax import lax
from jax.experimental import pallas as pl
from jax.experimental.pallas import tpu as pltpu

D_MODEL = 1024
DEPTH = 2
HEAD_DIM = 64
GROUP_WIDTH = D_MODEL // 4
N_HEADS = GROUP_WIDTH // HEAD_DIM
DIFF_QK_DIM = HEAD_DIM // 2
MLA_NOPE_DIM = HEAD_DIM
MLA_ROPE_DIM = HEAD_DIM // 2
MLA_V_DIM = HEAD_DIM
MLA_QK_DIM = MLA_NOPE_DIM + MLA_ROPE_DIM
MLA_Q_RANK = N_HEADS * MLA_QK_DIM
MLA_KV_RANK = 4 * MLA_V_DIM
GQA_KV_HEADS = N_HEADS // 2
GRID_W = 64
DIL_PAIRS = ((128, 1), (512, 4), (2048, 16))
MLP_HIDDEN = 4 * D_MODEL
ROPE_THETA = 10000.0
RMS_EPS = 1e-6
LN_EPS = 1e-5
NEG_INF = -1e30
LOG2E = math.log2(math.e)

LANES = 128
VMEM_LIMIT_BYTES = 56 * 1024 * 1024

PROJ_ROWS = 512
ATT_TQ = 512
ATT_TK = 512
ROW_TILE = 512
MLP_CHUNK = 1024
AUG_LANE = DIFF_QK_DIM
LOG2E_TERMS = 5
POS_SPLIT = 64
DIFF_PAIR = 2
DIL_BAND = 1024

_NT = (((1,), (1,)), ((), ()))

F32 = jnp.float32
BF16 = jnp.bfloat16


def _params(**kw):
    return pltpu.CompilerParams(vmem_limit_bytes=VMEM_LIMIT_BYTES, **kw)


def _const_spec(shape):
    nd = len(shape)
    return pl.BlockSpec(shape, lambda *_: (0,) * nd, pipeline_mode=pl.Buffered(1))


def _ada_kernel(c_ref, w_ref, b_ref, o_ref):
    c = c_ref[...]
    cond = c * (1.0 / (1.0 + jnp.exp(-c)))
    o_ref[0, 0] = jnp.dot(cond, w_ref[0], preferred_element_type=F32) + b_ref[0, 0]


def _ada_call(c, w_ada, b_ada):
    n_layers, d, _ = w_ada.shape
    bsz = c.shape[0]
    b4 = b_ada.reshape(n_layers, 6, 1, d)
    out = pl.pallas_call(
        _ada_kernel,
        out_shape=jax.ShapeDtypeStruct((n_layers, 6, bsz, d), F32),
        grid=(n_layers, 6),
        in_specs=[
            pl.BlockSpec((bsz, d), lambda l, j: (0, 0)),
            pl.BlockSpec((1, d, d), lambda l, j: (l, 0, j)),
            pl.BlockSpec((1, 1, 1, d), lambda l, j: (l, j, 0, 0)),
        ],
        out_specs=pl.BlockSpec((1, 1, bsz, d), lambda l, j: (l, j, 0, 0)),
        compiler_params=_params(),
        name="adaln_mod",
    )(c, w_ada, b4)
    return jnp.transpose(out, (0, 2, 1, 3))


_SEG_WIDTHS = (
    ("a_q", 256), ("a_k", 256), ("b_cq", 384), ("b_ckv", 256), ("b_kr", 128),
    ("c_q", 256), ("c_q_sw", 256), ("c_k", 128), ("c_k_sw", 128),
    ("d_q", 256), ("d_k", 256), ("v_a", 512), ("v_c", 256), ("v_d", 512),
)
_SEG = {}
_off = 0
for _name, _w in _SEG_WIDTHS:
    _SEG[_name] = (_off, _off + _w)
    _off += _w
W1_COLS = _off


def _lane_spread(x, mat):
    hi = x.astype(BF16)
    lo = (x - hi.astype(F32)).astype(BF16)
    return (jnp.dot(hi, mat, preferred_element_type=F32)
            + jnp.dot(lo, mat, preferred_element_type=F32))


def _rms_rows(x, gsum):
    return lax.rsqrt(_lane_spread(x * x, gsum) + RMS_EPS)


def _proj_kernel(x_ref, mod_ref, w1_ref, pa_ref, wuq_ref, wukv_ref, e_ref, g_ref, pcq_ref, pck_ref,
                 vec_ref, tb_ref, tc_ref, tck_ref, tkr_ref, qaug_ref, kaug_ref,
                 qa_ref, ka_ref, va_ref, qb_ref, kb_ref, vb_ref, qc_ref, kc_ref, vc_ref,
                 qd_ref, kd_ref, vd_ref):
    sh = mod_ref[0, 0:1, :]
    sc = mod_ref[0, 1:2, :]
    h = (x_ref[0] * (1.0 + sc) + sh).astype(BF16)

    def seg(name):
        a, b = _SEG[name]
        return jnp.dot(h, w1_ref[:, a:b], preferred_element_type=F32)

    def vec(row, width):
        return vec_ref[row:row + 1, 0:width]

    a_scale = DIFF_QK_DIM ** -0.5 * LOG2E
    qa = jnp.dot((seg("a_q") * a_scale).astype(BF16), pa_ref[...], preferred_element_type=F32)
    qa_ref[0] = qa.astype(BF16) + qaug_ref[...]
    ka = jnp.dot(seg("a_k").astype(BF16), pa_ref[...], preferred_element_type=F32)
    ka_ref[0] = ka.astype(BF16) + kaug_ref[...]
    va_ref[0] = (seg("v_a") + vec(0, 512)).astype(BF16)

    cq = seg("b_cq")
    r = lax.rsqrt(jnp.mean(cq * cq, axis=-1, keepdims=True) + RMS_EPS)
    cqn = (cq * r * vec(1, 384)).astype(BF16)
    q2 = jnp.dot(cqn, wuq_ref[...], preferred_element_type=F32)
    qb_ref[0] = (q2[:, 0:512] * tb_ref[0] + q2[:, 512:1024] * tb_ref[1]).astype(BF16)
    ckv = seg("b_ckv")
    r = lax.rsqrt(jnp.mean(ckv * ckv, axis=-1, keepdims=True) + RMS_EPS)
    ckvn = (ckv * r * vec(2, 256)).astype(BF16)
    kv2 = jnp.dot(ckvn, wukv_ref[...], preferred_element_type=F32)
    kr = (seg("b_kr") * tkr_ref[...]).astype(BF16)
    kb_ref[0] = (kv2[:, 0:512] + jnp.dot(kr, e_ref[...], preferred_element_type=F32)).astype(BF16)
    vb_ref[0] = (kv2[:, 512:1024] + vec(0, 512)).astype(BF16)

    cqx = seg("c_q")
    r = _rms_rows(cqx, g_ref[...])
    qc = cqx * r * vec(3, 256) * tc_ref[0] + seg("c_q_sw") * r * vec(4, 256) * tc_ref[1]
    qc_ref[0] = jnp.dot(qc.astype(BF16), pcq_ref[...], preferred_element_type=F32).astype(BF16)
    ckx = seg("c_k")
    r = _rms_rows(ckx, g_ref[0:128, 0:128])
    kc = ckx * r * vec(5, 128) * tck_ref[0] + seg("c_k_sw") * r * vec(6, 128) * tck_ref[1]
    kc_ref[0] = jnp.dot(kc.astype(BF16), pck_ref[...], preferred_element_type=F32).astype(BF16)
    vc_ref[0] = (seg("v_c") + vec(0, 256)).astype(BF16)

    d_scale = HEAD_DIM ** -0.5 * LOG2E
    qd = jnp.dot((seg("d_q") * d_scale).astype(BF16), pcq_ref[...], preferred_element_type=F32)
    qd_ref[0] = qd.astype(BF16)
    kd_ref[0] = jnp.dot(seg("d_k").astype(BF16), pcq_ref[...], preferred_element_type=F32).astype(BF16)
    vd_ref[0] = (seg("v_d") + vec(0, 512)).astype(BF16)


def _proj_call(x, mod_l, w1, consts, layer_w, tables):
    bsz, seq, d = x.shape
    tm = PROJ_ROWS
    nt = seq // tm
    out_widths = (1024, 1024, 512, 512, 512, 512, 512, 256, 256, 512, 512, 512)
    row_spec = lambda w: pl.BlockSpec((1, tm, w), lambda b, i: (b, i, 0))
    tab3 = lambda w: pl.BlockSpec((2, tm, w), lambda b, i: (0, i, 0))
    tab2 = lambda w: pl.BlockSpec((tm, w), lambda b, i: (i, 0))
    in_specs = [
        row_spec(d),
        pl.BlockSpec((1, 6, d), lambda b, i: (b, 0, 0)),
        _const_spec(w1.shape),
        _const_spec(consts["pa"].shape),
        _const_spec(layer_w["wuq"].shape),
        _const_spec(layer_w["wukv"].shape),
        _const_spec(consts["e"].shape),
        _const_spec(consts["g"].shape),
        _const_spec(consts["pcq"].shape),
        _const_spec(consts["pck"].shape),
        _const_spec(layer_w["vec"].shape),
        tab3(512), tab3(256), tab3(128), tab2(128), tab2(1024), tab2(1024),
    ]
    return pl.pallas_call(
        _proj_kernel,
        out_shape=tuple(jax.ShapeDtypeStruct((bsz, seq, w), BF16) for w in out_widths),
        grid=(bsz, nt),
        in_specs=in_specs,
        out_specs=tuple(row_spec(w) for w in out_widths),
        compiler_params=_params(),
        name="proj_prep",
    )(x, mod_l, w1, consts["pa"], layer_w["wuq"], layer_w["wukv"], consts["e"], consts["g"],
      consts["pcq"], consts["pck"], layer_w["vec"], tables["tb"], tables["tc"], tables["tck"],
      tables["tkr"], tables["qaug"], tables["kaug"])


def _softmax_steps(scores, vs, carry):
    stats = []
    for s, (m, _) in zip(scores, carry):
        m_new = jnp.maximum(m, jnp.max(s, axis=-1, keepdims=True))
        stats.append((m_new, jnp.exp2((s - m_new).astype(BF16)), jnp.exp2(m - m_new)))
    out = []
    for (m_new, p, alpha), v, (_, acc) in zip(stats, vs, carry):
        out.append((m_new, alpha * acc + jnp.dot(p, v, preferred_element_type=F32)))
    return tuple(out)


def _softmax_init(tq):
    return jnp.full((tq, 1), NEG_INF, F32), jnp.zeros((tq, LANES), F32)


def _normalize(acc):
    return acc[:, 0:HEAD_DIM] / acc[:, HEAD_DIM:HEAD_DIM + 1]


def _chunk(ref, off, group):
    return ref[0, pl.ds(off, ATT_TK), LANES * group:LANES * (group + 1)]


def _attn_plain_kernel(q_ref, k_ref, v_ref, o_ref, *, n_heads, q_per_kv):
    tq = q_ref.shape[1]
    nk = k_ref.shape[1] // ATT_TK
    carry = tuple(_softmax_init(tq) for _ in range(n_heads))
    for j in range(nk):
        off = j * ATT_TK
        scores = [lax.dot_general(q_ref[0, :, LANES * h:LANES * (h + 1)], _chunk(k_ref, off, h // q_per_kv),
                                  _NT, preferred_element_type=F32) for h in range(n_heads)]
        vs = [_chunk(v_ref, off, h // q_per_kv) for h in range(n_heads)]
        carry = _softmax_steps(scores, vs, carry)
    o_ref[0] = jnp.concatenate([_normalize(acc) for _, acc in carry], axis=1).astype(o_ref.dtype)


def _attn_plain_call(q, k, v, *, n_heads, q_per_kv):
    bsz, seq, qw = q.shape
    kw = k.shape[2]
    kern = functools.partial(_attn_plain_kernel, n_heads=n_heads, q_per_kv=q_per_kv)
    return pl.pallas_call(
        kern,
        out_shape=jax.ShapeDtypeStruct((bsz, seq, n_heads * HEAD_DIM), BF16),
        grid=(bsz, seq // ATT_TQ),
        in_specs=[
            pl.BlockSpec((1, ATT_TQ, qw), lambda b, i: (b, i, 0)),
            pl.BlockSpec((1, seq, kw), lambda b, i: (b, 0, 0)),
            pl.BlockSpec((1, seq, kw), lambda b, i: (b, 0, 0)),
        ],
        out_specs=pl.BlockSpec((1, ATT_TQ, n_heads * HEAD_DIM), lambda b, i: (b, i, 0)),
        compiler_params=_params(),
        name="attn_plain",
    )(q, k, v)


def _attn_diff_kernel(q_ref, k_ref, v_ref, lam_ref, g_ref, bias_ref, sel_ref, o_ref):
    tq = q_ref.shape[1]
    nk = k_ref.shape[1] // ATT_TK
    qi = pl.program_id(2)
    n_comp = 2 * DIFF_PAIR
    lane = lax.broadcasted_iota(jnp.int32, (1, LANES), 1)

    carry = tuple(_softmax_init(tq) for _ in range(n_comp))
    for j in range(nk):
        aug_sign = jnp.where(j < qi, 1.0, jnp.where(j > qi, -1.0, 0.0))
        aug = jnp.where(lane < AUG_LANE, 1.0, aug_sign).astype(BF16)
        off = j * ATT_TK
        scores = []
        for hc in range(n_comp):
            q = q_ref[0, :, LANES * hc:LANES * (hc + 1)] * aug
            s = lax.dot_general(q, _chunk(k_ref, off, hc), _NT, preferred_element_type=F32)
            scores.append(s + bias_ref[0, jnp.where(j == qi, hc // 2, DIFF_PAIR)])
        vs = [_chunk(v_ref, off, hc // 2) for hc in range(n_comp)]
        carry = _softmax_steps(scores, vs, carry)

    lf = lam_ref[...]
    lambda_init = lf[4:5, 0:1]
    lam = (jnp.exp(jnp.sum(lf[0:1] * lf[1:2], axis=-1, keepdims=True))
           - jnp.exp(jnp.sum(lf[2:3] * lf[3:4], axis=-1, keepdims=True)) + lambda_init)
    outs = []
    for h in range(DIFF_PAIR):
        a0, a1 = carry[2 * h][1], carry[2 * h + 1][1]
        y = a0 / _lane_spread(a0, sel_ref[0]) - lam * (a1 / _lane_spread(a1, sel_ref[0]))
        r = lax.rsqrt(_lane_spread(y * y, sel_ref[1]) + RMS_EPS)
        outs.append((y * r * g_ref[...] * (1.0 - lambda_init))[:, 0:HEAD_DIM])
    o_ref[0] = jnp.concatenate(outs, axis=1).astype(o_ref.dtype)


def _attn_diff_call(q, k, v, lam, subln_g, bias, sel):
    bsz, seq, _ = q.shape
    qw = 2 * DIFF_PAIR * LANES
    vw = DIFF_PAIR * LANES
    ow = DIFF_PAIR * HEAD_DIM
    return pl.pallas_call(
        _attn_diff_kernel,
        out_shape=jax.ShapeDtypeStruct((bsz, seq, GROUP_WIDTH), BF16),
        grid=(bsz, N_HEADS // DIFF_PAIR, seq // ATT_TQ),
        in_specs=[
            pl.BlockSpec((1, ATT_TQ, qw), lambda b, p, i: (b, i, p)),
            pl.BlockSpec((1, seq, qw), lambda b, p, i: (b, 0, p)),
            pl.BlockSpec((1, seq, vw), lambda b, p, i: (b, 0, p)),
            _const_spec(lam.shape),
            _const_spec(subln_g.shape),
            pl.BlockSpec((1, DIFF_PAIR + 1, ATT_TQ, ATT_TK), lambda b, p, i: (p, 0, 0, 0)),
            _const_spec(sel.shape),
        ],
        out_specs=pl.BlockSpec((1, ATT_TQ, ow), lambda b, p, i: (b, i, p)),
        compiler_params=_params(),
        name="attn_diff",
    )(q, k, v, lam, subln_g, bias, sel)


def _attn_dil_kernel(q_ref, k_ref, v_ref, bias_ref, o_ref):
    tq = q_ref.shape[1]
    nk = k_ref.shape[1] // ATT_TK
    reach = DIL_BAND // ATT_TK
    qi = pl.program_id(1)
    n_off = 2 * reach + 1
    carry = tuple(_softmax_init(tq) for _ in range(N_HEADS))
    for t in [reach] + [t for t in range(n_off) if t != reach]:
        j = qi + (t - reach)
        valid = jnp.logical_and(j >= 0, j < nk)
        off = pl.multiple_of(jnp.clip(j, 0, nk - 1) * ATT_TK, ATT_TK)
        tb = jnp.where(valid, t, n_off)
        scores = [lax.dot_general(q_ref[0, :, LANES * h:LANES * (h + 1)], _chunk(k_ref, off, h), _NT,
                                  preferred_element_type=F32) + bias_ref[h, tb] for h in range(N_HEADS)]
        vs = [_chunk(v_ref, off, h) for h in range(N_HEADS)]
        carry = _softmax_steps(scores, vs, carry)
    o_ref[0] = jnp.concatenate([_normalize(acc) for _, acc in carry], axis=1).astype(o_ref.dtype)


def _attn_dil_call(q, k, v, bias):
    bsz, seq, qw = q.shape
    return pl.pallas_call(
        _attn_dil_kernel,
        out_shape=jax.ShapeDtypeStruct((bsz, seq, GROUP_WIDTH), BF16),
        grid=(bsz, seq // ATT_TQ),
        in_specs=[
            pl.BlockSpec((1, ATT_TQ, qw), lambda b, i: (b, i, 0)),
            pl.BlockSpec((1, seq, qw), lambda b, i: (b, 0, 0)),
            pl.BlockSpec((1, seq, qw), lambda b, i: (b, 0, 0)),
            _const_spec(bias.shape),
        ],
        out_specs=pl.BlockSpec((1, ATT_TQ, GROUP_WIDTH), lambda b, i: (b, i, 0)),
        compiler_params=_params(),
        name="attn_dilated",
    )(q, k, v, bias)


def _layer_norm(z, g, b):
    mu = jnp.mean(z, axis=-1, keepdims=True)
    zc = z - mu
    var = jnp.mean(zc * zc, axis=-1, keepdims=True)
    return zc * lax.rsqrt(var + LN_EPS) * g + b


def _outproj_kernel(ya_ref, yb_ref, yc_ref, yd_ref, x_ref, mod_ref, wo_ref, ln_ref, o_ref, *, alpha):
    half = x_ref.shape[1] // 2
    for r in (0, half):
        rows = slice(r, r + half)
        ycat = jnp.concatenate([ya_ref[0, rows], yb_ref[0, rows], yc_ref[0, rows], yd_ref[0, rows]], axis=1)
        y = jnp.dot(ycat, wo_ref[...], preferred_element_type=F32)
        z = alpha * x_ref[0, rows] + mod_ref[0, 2:3, :] * y
        o_ref[0, rows] = _layer_norm(z, ln_ref[0:1, :], ln_ref[1:2, :])


def _outproj_call(ys, x, mod_l, wo, ln, *, alpha):
    bsz, seq, d = x.shape
    tm = ROW_TILE
    yspec = pl.BlockSpec((1, tm, GROUP_WIDTH), lambda b, i: (b, i, 0))
    xspec = pl.BlockSpec((1, tm, d), lambda b, i: (b, i, 0))
    return pl.pallas_call(
        functools.partial(_outproj_kernel, alpha=alpha),
        out_shape=jax.ShapeDtypeStruct((bsz, seq, d), F32),
        grid=(bsz, seq // tm),
        in_specs=[yspec, yspec, yspec, yspec, xspec,
                  pl.BlockSpec((1, 6, d), lambda b, i: (b, 0, 0)),
                  _const_spec(wo.shape), _const_spec(ln.shape)],
        out_specs=xspec,
        compiler_params=_params(),
        name="out_proj_ln",
    )(*ys, x, mod_l, wo, ln)


def _mlp_kernel(x_ref, mod_ref, wup_ref, wdn_ref, ln_ref, o_ref, *, alpha):
    x = x_ref[0]
    h = (x * (1.0 + mod_ref[0, 4:5, :]) + mod_ref[0, 3:4, :]).astype(BF16)
    u = jnp.zeros(x.shape, F32)
    for c in range(MLP_HIDDEN // MLP_CHUNK):
        a = jnp.dot(h, wup_ref[:, c * MLP_CHUNK:(c + 1) * MLP_CHUNK], preferred_element_type=F32)
        a = jnp.maximum(a, 0.0)
        u = u + jnp.dot((a * a).astype(BF16), wdn_ref[c * MLP_CHUNK:(c + 1) * MLP_CHUNK, :],
                        preferred_element_type=F32)
    z = alpha * x + mod_ref[0, 5:6, :] * u
    o_ref[0] = _layer_norm(z, ln_ref[0:1, :], ln_ref[1:2, :])


def _mlp_call(x, mod_l, wup, wdn, ln, *, alpha):
    bsz, seq, d = x.shape
    tm = ROW_TILE
    xspec = pl.BlockSpec((1, tm, d), lambda b, i: (b, i, 0))
    return pl.pallas_call(
        functools.partial(_mlp_kernel, alpha=alpha),
        out_shape=jax.ShapeDtypeStruct((bsz, seq, d), F32),
        grid=(bsz, seq // tm),
        in_specs=[xspec, pl.BlockSpec((1, 6, d), lambda b, i: (b, 0, 0)),
                  _const_spec(wup.shape), _const_spec(wdn.shape), _const_spec(ln.shape)],
        out_specs=xspec,
        compiler_params=_params(),
        name="mlp_ln",
    )(x, mod_l, wup, wdn, ln)


def _rot_half_source(width, block):
    half = block // 2
    d = np.arange(width)
    inner = d % block
    src = np.where(inner < half, d + half, d - half)
    sign = np.where(inner < half, -1.0, 1.0)
    return src, sign.astype(np.float32)


def _placement(n_in, n_out, pairs):
    m = np.zeros((n_in, n_out), np.float32)
    for r, c in pairs:
        m[r, c] = 1.0
    return jnp.asarray(m, BF16)


def _static_consts():
    pa = _placement(256, 1024, [(32 * hc + d, LANES * hc + d) for hc in range(8) for d in range(32)])
    pcq = _placement(256, 512, [(64 * h + d, LANES * h + d) for h in range(4) for d in range(64)])
    pck = _placement(128, 256, [(64 * g + d, LANES * g + d) for g in range(2) for d in range(64)])
    e = _placement(128, 512, [(s * 32 + d, LANES * h + MLA_NOPE_DIM + d)
                              for h in range(4) for s in range(2) for d in range(32)])
    g = np.zeros((256, 256), np.float32)
    for h in range(4):
        g[64 * h:64 * (h + 1), 64 * h:64 * (h + 1)] = 1.0 / HEAD_DIM
    sel = np.zeros((2, LANES, LANES), np.float32)
    sel[0, HEAD_DIM, :] = 1.0
    sel[1, 0:HEAD_DIM, 0:HEAD_DIM] = 1.0 / HEAD_DIM
    return dict(pa=pa, pcq=pcq, pck=pck, e=e, g=jnp.asarray(g, BF16), sel=jnp.asarray(sel, BF16))


def _rope_tables(pos, half):
    freqs = ROPE_THETA ** (-jnp.arange(half, dtype=F32) / half)
    ang = pos[:, None] * freqs[None, :]
    cos = jnp.concatenate([jnp.cos(ang), jnp.cos(ang)], axis=1)
    sin = jnp.concatenate([jnp.sin(ang), jnp.sin(ang)], axis=1)
    return cos, sin


def _position_tables(seq, slopes_a, slopes_d):
    pos = jnp.arange(seq, dtype=F32)
    row = jnp.floor(pos / GRID_W)
    col = pos - row * GRID_W
    cos32, sin32 = _rope_tables(pos, MLA_ROPE_DIM // 2)
    zeros32 = jnp.zeros((seq, 32), F32)
    ones64 = jnp.ones((seq, 64), F32)
    zeros64 = jnp.zeros((seq, 64), F32)
    b_scale = MLA_QK_DIM ** -0.5 * LOG2E
    cos_b = jnp.tile(jnp.concatenate([ones64, cos32, zeros32], 1), (1, 4)) * b_scale
    sin_b = jnp.tile(jnp.concatenate([zeros64, sin32, zeros32], 1), (1, 4)) * b_scale
    tkr = jnp.concatenate([cos32, sin32, zeros64], 1)
    cr, sr = _rope_tables(row, HEAD_DIM // 4)
    cc, sc = _rope_tables(col, HEAD_DIM // 4)
    cos_h = jnp.concatenate([cr, cc], 1)
    sin_h = jnp.concatenate([sr, sc], 1)
    c_scale = HEAD_DIM ** -0.5 * LOG2E
    tc = jnp.stack([jnp.tile(cos_h, (1, 4)), jnp.tile(sin_h, (1, 4))]) * c_scale
    tck = jnp.stack([jnp.tile(cos_h, (1, 2)), jnp.tile(sin_h, (1, 2))])
    ipos = np.arange(seq)
    hi = (ipos // POS_SPLIT).astype(np.float32)
    lo = (ipos % POS_SPLIT).astype(np.float32)
    qaug = np.zeros((seq, 8 * LANES), np.float32)
    kaug = np.zeros((seq, 8 * LANES), np.float32)
    terms, rest = [], LOG2E
    for _ in range(LOG2E_TERMS):
        t = float(np.float32(rest).astype(ml_dtypes.bfloat16))
        terms.append(t)
        rest -= t
    for hc in range(8):
        s = float(slopes_a[hc // 2])
        for n, t in enumerate(terms):
            base = LANES * hc + AUG_LANE + 4 * n
            qaug[:, base + 0] = hi
            qaug[:, base + 1] = lo
            qaug[:, base + 2] = s * t * POS_SPLIT
            qaug[:, base + 3] = s * t
            kaug[:, base + 0] = -s * t * POS_SPLIT
            kaug[:, base + 1] = -s * t
            kaug[:, base + 2] = hi
            kaug[:, base + 3] = lo
    reach = DIL_BAND // ATT_TK
    a = np.arange(ATT_TQ)[:, None]
    b = np.arange(ATT_TK)[None, :]
    tabs = []
    for t in range(2 * reach + 1):
        o = (t - reach) * ATT_TK + b - a
        cnt = np.zeros(o.shape, np.float32)
        for w, dil in DIL_PAIRS:
            cnt += ((np.abs(o) <= w // 2) & (o % dil == 0)).astype(np.float32)
        tabs.append((np.abs(o).astype(np.float32), cnt))
    dist = np.stack([t[0] for t in tabs])
    cnt = np.stack([t[1] for t in tabs])
    logc = np.where(cnt > 0, np.log(np.maximum(cnt, 1.0)), NEG_INF).astype(np.float32)
    bias_d = np.stack([np.where(cnt > 0, (-float(s) * dist + logc) * LOG2E, NEG_INF) for s in slopes_d])
    bias_d = np.concatenate([bias_d, np.full_like(bias_d[:, :1], NEG_INF)], axis=1)
    bias_a = np.stack([(-float(s) * LOG2E * np.abs(a - b)).astype(np.float32) for s in slopes_a])
    bias_a = bias_a.reshape(N_HEADS // DIFF_PAIR, DIFF_PAIR, ATT_TQ, ATT_TK)
    bias_a = np.concatenate([bias_a, np.zeros_like(bias_a[:, :1])], axis=1)
    return dict(tb=jnp.stack([cos_b, sin_b]), tc=tc, tck=tck, tkr=tkr, bias_a=jnp.asarray(bias_a),
                qaug=jnp.asarray(qaug, BF16), kaug=jnp.asarray(kaug, BF16),
                bias_d=jnp.asarray(bias_d.astype(np.float32)))


def _place_cols(w, n_heads, src_stride, src_off, width, dst_width=LANES):
    parts = []
    for h in range(n_heads):
        blk = w[:, h * src_stride + src_off:h * src_stride + src_off + width]
        parts.append(jnp.pad(blk, ((0, 0), (0, dst_width - width))))
    return jnp.concatenate(parts, axis=1)


def _layer_weights(w_in, mla_w_uq, mla_w_ukv, mla_q_norm_g, mla_kv_norm_g, gqa_q_norm_g, gqa_k_norm_g):
    o_aq, o_ak, o_av, o_cq, o_ckv, o_kr = 0, 256, 512, 768, 1152, 1408
    o_gq, o_gk, o_gv, o_dq, o_dk, o_dv = 1440, 1696, 1824, 1952, 2208, 2464
    src32, sign32 = _rot_half_source(32, 32)
    kr = w_in[:, o_kr:o_kr + 32]
    kr_sw = kr[:, src32] * sign32
    srcq, signq = _rot_half_source(256, 32)
    gq = w_in[:, o_gq:o_gq + 256]
    gk = w_in[:, o_gk:o_gk + 128]
    w1 = jnp.concatenate([
        w_in[:, o_aq:o_aq + 256], w_in[:, o_ak:o_ak + 256],
        w_in[:, o_cq:o_cq + 384], w_in[:, o_ckv:o_ckv + 256],
        kr, kr_sw, jnp.zeros((w_in.shape[0], 64), w_in.dtype),
        gq, gq[:, srcq] * signq, gk, gk[:, srcq[:128]] * signq[:128],
        w_in[:, o_dq:o_dq + 256], w_in[:, o_dk:o_dk + 256],
        _place_cols(w_in[:, o_av:o_av + 256], 4, 64, 0, 64),
        _place_cols(w_in[:, o_gv:o_gv + 128], 2, 64, 0, 64),
        _place_cols(w_in[:, o_dv:o_dv + 256], 4, 64, 0, 64),
    ], axis=1).astype(BF16)
    assert w1.shape[1] == W1_COLS
    uq_plain = _place_cols(mla_w_uq, 4, MLA_QK_DIM, 0, MLA_QK_DIM)
    pe = [mla_w_uq[:, h * MLA_QK_DIM + MLA_NOPE_DIM:(h + 1) * MLA_QK_DIM] for h in range(4)]
    uq_sw = jnp.concatenate([
        jnp.pad(p[:, src32] * sign32, ((0, 0), (MLA_NOPE_DIM, LANES - MLA_QK_DIM))) for p in pe], axis=1)
    wuq = jnp.concatenate([uq_plain, uq_sw], axis=1).astype(BF16)
    uk = _place_cols(mla_w_ukv, 4, MLA_NOPE_DIM + MLA_V_DIM, 0, MLA_NOPE_DIM)
    uv = _place_cols(mla_w_ukv, 4, MLA_NOPE_DIM + MLA_V_DIM, MLA_NOPE_DIM, MLA_V_DIM)
    wukv = jnp.concatenate([uk, uv], axis=1).astype(BF16)
    ones = np.zeros((512,), np.float32)
    ones[HEAD_DIM::LANES] = 1.0
    src64, _ = _rot_half_source(64, 32)
    gq_g = jnp.tile(gqa_q_norm_g, 4)
    gq_gs = jnp.tile(gqa_q_norm_g[src64], 4)
    gk_g = jnp.tile(gqa_k_norm_g, 2)
    gk_gs = jnp.tile(gqa_k_norm_g[src64], 2)
    pad = lambda v: jnp.pad(v.astype(F32), (0, 512 - v.shape[0]))
    vec = jnp.stack([jnp.asarray(ones), pad(mla_q_norm_g), pad(mla_kv_norm_g),
                     pad(gq_g), pad(gq_gs), pad(gk_g), pad(gk_gs), jnp.zeros((512,), F32)])
    return w1, dict(wuq=wuq, wukv=wukv, vec=vec)


def kernel(x, c, w_ada, b_ada, w_in, w_o, diff_lambda, diff_subln_g, mla_q_norm_g, mla_w_uq, mla_kv_norm_g, mla_w_ukv, gqa_q_norm_g, gqa_k_norm_g, ln_attn_g, ln_attn_b, w_up, w_down, ln_mlp_g, ln_mlp_b):
    alpha = (2 * DEPTH) ** 0.25
    seq = x.shape[1]
    n_slopes = 2 * N_HEADS
    slopes = [2.0 ** (-8.0 * (n + 1.0) / n_slopes) for n in range(n_slopes)]
    slopes_a, slopes_d = tuple(slopes[0::2]), tuple(slopes[1::2])
    consts = _static_consts()
    tables = _position_tables(seq, slopes_a, slopes_d)
    mod = _ada_call(c, w_ada, b_ada)
    for l in range(DEPTH):
        lambda_init = 0.8 - 0.6 * math.exp(-0.3 * l)
        w1, layer_w = _layer_weights(w_in[l], mla_w_uq[l], mla_w_ukv[l], mla_q_norm_g[l],
                                     mla_kv_norm_g[l], gqa_q_norm_g[l], gqa_k_norm_g[l])
        (qa, ka, va, qb, kb, vb, qc, kc, vc, qd, kd, vd) = _proj_call(
            x, mod[l], w1, consts, layer_w, tables)
        lam_pack = jnp.concatenate([diff_lambda[l].astype(F32),
                                    jnp.full((1, DIFF_QK_DIM), lambda_init, F32)], axis=0)
        subln_g = jnp.pad(diff_subln_g[l].astype(F32), (0, LANES - HEAD_DIM)).reshape(1, LANES)
        ya = _attn_diff_call(qa, ka, va, lam_pack, subln_g, tables["bias_a"], consts["sel"])
        yb = _attn_plain_call(qb, kb, vb, n_heads=N_HEADS, q_per_kv=1)
        yc = _attn_plain_call(qc, kc, vc, n_heads=N_HEADS, q_per_kv=N_HEADS // GQA_KV_HEADS)
        yd = _attn_dil_call(qd, kd, vd, tables["bias_d"])
        x = _outproj_call((ya, yb, yc, yd), x, mod[l], w_o[l].astype(BF16),
                          jnp.stack([ln_attn_g[l], ln_attn_b[l]]), alpha=alpha)
        x = _mlp_call(x, mod[l], w_up[l].astype(BF16), w_down[l].astype(BF16),
                      jnp.stack([ln_mlp_g[l], ln_mlp_b[l]]), alpha=alpha)
    return x
```

```python
import functools
import math

import jax
import jax.numpy as jnp
import ml_dtypes
import numpy as np
from jax import lax
from jax.experimental import pallas as pl
from jax.experimental.pallas import tpu as pltpu

D_MODEL = 1024
DEPTH = 2
HEAD_DIM = 64
GROUP_WIDTH = D_MODEL // 4
N_HEADS = GROUP_WIDTH // HEAD_DIM
DIFF_QK_DIM = HEAD_DIM // 2
MLA_NOPE_DIM = HEAD_DIM
MLA_ROPE_DIM = HEAD_DIM // 2
MLA_V_DIM = HEAD_DIM
MLA_QK_DIM = MLA_NOPE_DIM + MLA_ROPE_DIM
MLA_Q_RANK = N_HEADS * MLA_QK_DIM
MLA_KV_RANK = 4 * MLA_V_DIM
GQA_KV_HEADS = N_HEADS // 2
GRID_W = 64
DIL_PAIRS = ((128, 1), (512, 4), (2048, 16))
MLP_HIDDEN = 4 * D_MODEL
ROPE_THETA = 10000.0
RMS_EPS = 1e-6
LN_EPS = 1e-5
NEG_INF = -1e30
LOG2E = math.log2(math.e)

LANES = 128
VMEM_LIMIT_BYTES = 56 * 1024 * 1024

PROJ_ROWS = 512
ATT_TQ = 512
ATT_TK = 512
ROW_TILE = 512
MLP_CHUNK = 1024
AUG_LANE = DIFF_QK_DIM
LOG2E_TERMS = 5
POS_SPLIT = 64
DIFF_PAIR = 2
DIL_BAND = 1024

_NT = (((1,), (1,)), ((), ()))

F32 = jnp.float32
BF16 = jnp.bfloat16


def _params(**kw):
    return pltpu.CompilerParams(vmem_limit_bytes=VMEM_LIMIT_BYTES, **kw)


def _const_spec(shape):
    nd = len(shape)
    return pl.BlockSpec(shape, lambda *_: (0,) * nd, pipeline_mode=pl.Buffered(1))


def _layer_spec(shape, layer):
    nd = len(shape)
    return pl.BlockSpec((None,) + tuple(shape[1:]), lambda *_: (layer,) + (0,) * (nd - 1),
                        pipeline_mode=pl.Buffered(1))


def _mod_spec(d, layer):
    return pl.BlockSpec((None, 1, 6, d), lambda b, *_: (layer, b, 0, 0))


def _ada_kernel(c_ref, w_ref, b_ref, o_ref):
    c = c_ref[...]
    cond = c * (1.0 / (1.0 + jnp.exp(-c)))
    o_ref[0] = jnp.dot(cond, w_ref[0], preferred_element_type=F32) + b_ref[0, 0]


def _ada_call(c, w_ada, b_ada):
    n_layers, d, _ = w_ada.shape
    bsz = c.shape[0]
    b4 = b_ada.reshape(n_layers, 6, 1, d)
    out = pl.pallas_call(
        _ada_kernel,
        out_shape=jax.ShapeDtypeStruct((n_layers, bsz, 6 * d), F32),
        grid=(n_layers, 6),
        in_specs=[
            pl.BlockSpec((bsz, d), lambda l, j: (0, 0)),
            pl.BlockSpec((1, d, d), lambda l, j: (l, 0, j)),
            pl.BlockSpec((1, 1, 1, d), lambda l, j: (l, j, 0, 0)),
        ],
        out_specs=pl.BlockSpec((1, bsz, d), lambda l, j: (l, 0, j)),
        compiler_params=_params(),
        name="adaln_mod",
    )(c, w_ada, b4)
    return out.reshape(n_layers, bsz, 6, d)


_SEG_WIDTHS = (
    ("a_q", 256), ("a_k", 256), ("b_cq", 384), ("b_ckv", 256), ("b_kr", 128),
    ("c_q", 256), ("c_q_sw", 256), ("c_k", 128), ("c_k_sw", 128),
    ("d_q", 256), ("d_k", 256), ("v_a", 512), ("v_c", 256), ("v_d", 512),
)
_SEG = {}
_off = 0
for _name, _w in _SEG_WIDTHS:
    _SEG[_name] = (_off, _off + _w)
    _off += _w
W1_COLS = _off


def _lane_spread(x, mat):
    hi = x.astype(BF16)
    lo = (x - hi.astype(F32)).astype(BF16)
    return (jnp.dot(hi, mat, preferred_element_type=F32)
            + jnp.dot(lo, mat, preferred_element_type=F32))


def _rms_rows(x, gsum):
    return lax.rsqrt(_lane_spread(x * x, gsum) + RMS_EPS)


def _proj_kernel(x_ref, mod_ref, w1_ref, pa_ref, wuq_ref, wukv_ref, e_ref, g_ref, pcq_ref, pck_ref,
                 vec_ref, tb_ref, tc_ref, tck_ref, tkr_ref, qaug_ref, kaug_ref,
                 qa_ref, ka_ref, va_ref, qb_ref, kb_ref, vb_ref, qc_ref, kc_ref, vc_ref,
                 qd_ref, kd_ref, vd_ref):
    sh = mod_ref[0, 0:1, :]
    sc = mod_ref[0, 1:2, :]
    h = (x_ref[0] * (1.0 + sc) + sh).astype(BF16)

    def seg(name):
        a, b = _SEG[name]
        return jnp.dot(h, w1_ref[:, a:b], preferred_element_type=F32)

    def vec(row, width):
        return vec_ref[row:row + 1, 0:width]

    a_scale = DIFF_QK_DIM ** -0.5 * LOG2E
    qa = jnp.dot((seg("a_q") * a_scale).astype(BF16), pa_ref[...], preferred_element_type=F32)
    qa_ref[0] = qa.astype(BF16) + qaug_ref[...]
    ka = jnp.dot(seg("a_k").astype(BF16), pa_ref[...], preferred_element_type=F32)
    ka_ref[0] = ka.astype(BF16) + kaug_ref[...]
    va_ref[0] = (seg("v_a") + vec(0, 512)).astype(BF16)

    cq = seg("b_cq")
    r = lax.rsqrt(jnp.mean(cq * cq, axis=-1, keepdims=True) + RMS_EPS)
    cqn = (cq * r * vec(1, 384)).astype(BF16)
    q2 = jnp.dot(cqn, wuq_ref[...], preferred_element_type=F32)
    qb_ref[0] = (q2[:, 0:512] * tb_ref[0] + q2[:, 512:1024] * tb_ref[1]).astype(BF16)
    ckv = seg("b_ckv")
    r = lax.rsqrt(jnp.mean(ckv * ckv, axis=-1, keepdims=True) + RMS_EPS)
    ckvn = (ckv * r * vec(2, 256)).astype(BF16)
    kv2 = jnp.dot(ckvn, wukv_ref[...], preferred_element_type=F32)
    kr = (seg("b_kr") * tkr_ref[...]).astype(BF16)
    kb_ref[0] = (kv2[:, 0:512] + jnp.dot(kr, e_ref[...], preferred_element_type=F32)).astype(BF16)
    vb_ref[0] = (kv2[:, 512:1024] + vec(0, 512)).astype(BF16)

    cqx = seg("c_q")
    r = _rms_rows(cqx, g_ref[...])
    qc = cqx * r * vec(3, 256) * tc_ref[0] + seg("c_q_sw") * r * vec(4, 256) * tc_ref[1]
    qc_ref[0] = jnp.dot(qc.astype(BF16), pcq_ref[...], preferred_element_type=F32).astype(BF16)
    ckx = seg("c_k")
    r = _rms_rows(ckx, g_ref[0:128, 0:128])
    kc = ckx * r * vec(5, 128) * tck_ref[0] + seg("c_k_sw") * r * vec(6, 128) * tck_ref[1]
    kc_ref[0] = jnp.dot(kc.astype(BF16), pck_ref[...], preferred_element_type=F32).astype(BF16)
    vc_ref[0] = (seg("v_c") + vec(0, 256)).astype(BF16)

    d_scale = HEAD_DIM ** -0.5 * LOG2E
    qd = jnp.dot((seg("d_q") * d_scale).astype(BF16), pcq_ref[...], preferred_element_type=F32)
    qd_ref[0] = qd.astype(BF16)
    kd_ref[0] = jnp.dot(seg("d_k").astype(BF16), pcq_ref[...], preferred_element_type=F32).astype(BF16)
    vd_ref[0] = (seg("v_d") + vec(0, 512)).astype(BF16)


def _proj_call(x, mod, layer, consts, layer_w, tables):
    w1 = layer_w["w1"]
    bsz, seq, d = x.shape
    tm = PROJ_ROWS
    nt = seq // tm
    out_widths = (1024, 1024, 512, 512, 512, 512, 512, 256, 256, 512, 512, 512)
    row_spec = lambda w: pl.BlockSpec((1, tm, w), lambda b, i: (b, i, 0))
    tab3 = lambda w: pl.BlockSpec((2, tm, w), lambda b, i: (0, i, 0))
    tab2 = lambda w: pl.BlockSpec((tm, w), lambda b, i: (i, 0))
    in_specs = [
        row_spec(d),
        _mod_spec(d, layer),
        _layer_spec(w1.shape, layer),
        _const_spec(consts["pa"].shape),
        _layer_spec(layer_w["wuq"].shape, layer),
        _layer_spec(layer_w["wukv"].shape, layer),
        _const_spec(consts["e"].shape),
        _const_spec(consts["g"].shape),
        _const_spec(consts["pcq"].shape),
        _const_spec(consts["pck"].shape),
        _layer_spec(layer_w["vec"].shape, layer),
        tab3(512), tab3(256), tab3(128), tab2(128), tab2(1024), tab2(1024),
    ]
    return pl.pallas_call(
        _proj_kernel,
        out_shape=tuple(jax.ShapeDtypeStruct((bsz, seq, w), BF16) for w in out_widths),
        grid=(bsz, nt),
        in_specs=in_specs,
        out_specs=tuple(row_spec(w) for w in out_widths),
        compiler_params=_params(),
        name="proj_prep",
    )(x, mod, w1, consts["pa"], layer_w["wuq"], layer_w["wukv"], consts["e"], consts["g"],
      consts["pcq"], consts["pck"], layer_w["vec"], tables["tb"], tables["tc"], tables["tck"],
      tables["tkr"], tables["qaug"], tables["kaug"])


def _softmax_steps(scores, vs, carry):
    stats = []
    for s, (m, _) in zip(scores, carry):
        m_new = jnp.maximum(m, jnp.max(s, axis=-1, keepdims=True))
        stats.append((m_new, jnp.exp2(s - m_new).astype(BF16), jnp.exp2(m - m_new)))
    out = []
    for (m_new, p, alpha), v, (_, acc) in zip(stats, vs, carry):
        out.append((m_new, alpha * acc + jnp.dot(p, v, preferred_element_type=F32)))
    return tuple(out)


def _softmax_init(tq):
    return jnp.full((tq, 1), NEG_INF, F32), jnp.zeros((tq, LANES), F32)


def _normalize(acc):
    return acc[:, 0:HEAD_DIM] / acc[:, HEAD_DIM:HEAD_DIM + 1]


def _chunk(ref, off, group):
    return ref[0, pl.ds(off, ATT_TK), LANES * group:LANES * (group + 1)]


def _attn_plain_kernel(q_ref, k_ref, v_ref, o_ref, *, n_heads, q_per_kv):
    tq = q_ref.shape[1]
    nk = k_ref.shape[1] // ATT_TK
    carry = tuple(_softmax_init(tq) for _ in range(n_heads))
    for j in range(nk):
        off = j * ATT_TK
        scores = [lax.dot_general(q_ref[0, :, LANES * h:LANES * (h + 1)], _chunk(k_ref, off, h // q_per_kv),
                                  _NT, preferred_element_type=F32) for h in range(n_heads)]
        vs = [_chunk(v_ref, off, h // q_per_kv) for h in range(n_heads)]
        carry = _softmax_steps(scores, vs, carry)
    o_ref[0] = jnp.concatenate([_normalize(acc) for _, acc in carry], axis=1).astype(o_ref.dtype)


def _attn_plain_call(q, k, v, *, n_heads, q_per_kv):
    bsz, seq, qw = q.shape
    kw = k.shape[2]
    kern = functools.partial(_attn_plain_kernel, n_heads=n_heads, q_per_kv=q_per_kv)
    return pl.pallas_call(
        kern,
        out_shape=jax.ShapeDtypeStruct((bsz, seq, n_heads * HEAD_DIM), BF16),
        grid=(bsz, seq // ATT_TQ),
        in_specs=[
            pl.BlockSpec((1, ATT_TQ, qw), lambda b, i: (b, i, 0)),
            pl.BlockSpec((1, seq, kw), lambda b, i: (b, 0, 0)),
            pl.BlockSpec((1, seq, kw), lambda b, i: (b, 0, 0)),
        ],
        out_specs=pl.BlockSpec((1, ATT_TQ, n_heads * HEAD_DIM), lambda b, i: (b, i, 0)),
        compiler_params=_params(),
        name="attn_plain",
    )(q, k, v)


def _attn_diff_kernel(q_ref, k_ref, v_ref, lam_ref, g_ref, bias_ref, sel_ref, o_ref):
    tq = q_ref.shape[1]
    nk = k_ref.shape[1] // ATT_TK
    qi = pl.program_id(2)
    n_comp = 2 * DIFF_PAIR
    lane = lax.broadcasted_iota(jnp.int32, (1, LANES), 1)

    def step(off, carry, aug_sign, diagonal):
        aug = jnp.where(lane < AUG_LANE, 1.0, aug_sign).astype(BF16)
        scores = []
        for hc in range(n_comp):
            q = q_ref[0, :, LANES * hc:LANES * (hc + 1)] * aug
            s = lax.dot_general(q, _chunk(k_ref, off, hc), _NT, preferred_element_type=F32)
            if diagonal:
                s = s + bias_ref[hc // 2]
            scores.append(s)
        vs = [_chunk(v_ref, off, hc // 2) for hc in range(n_comp)]
        return _softmax_steps(scores, vs, carry)

    carry = step(pl.multiple_of(qi * ATT_TK, ATT_TK), tuple(_softmax_init(tq) for _ in range(n_comp)),
                 0.0, True)
    for j in range(nk - 1):
        jj = j + (j >= qi).astype(jnp.int32)
        sign = jnp.where(jj < qi, 1.0, -1.0)
        carry = step(pl.multiple_of(jj * ATT_TK, ATT_TK), carry, sign, False)

    lf = lam_ref[...]
    lambda_init = lf[4:5, 0:1]
    lam = (jnp.exp(jnp.sum(lf[0:1] * lf[1:2], axis=-1, keepdims=True))
           - jnp.exp(jnp.sum(lf[2:3] * lf[3:4], axis=-1, keepdims=True)) + lambda_init)
    outs = []
    for h in range(DIFF_PAIR):
        a0, a1 = carry[2 * h][1], carry[2 * h + 1][1]
        y = a0 / _lane_spread(a0, sel_ref[0]) - lam * (a1 / _lane_spread(a1, sel_ref[0]))
        r = lax.rsqrt(_lane_spread(y * y, sel_ref[1]) + RMS_EPS)
        outs.append((y * r * g_ref[...] * (1.0 - lambda_init))[:, 0:HEAD_DIM])
    o_ref[0] = jnp.concatenate(outs, axis=1).astype(o_ref.dtype)


def _attn_diff_call(q, k, v, layer, lam, subln_g, bias, sel):
    bsz, seq, _ = q.shape
    qw = 2 * DIFF_PAIR * LANES
    vw = DIFF_PAIR * LANES
    ow = DIFF_PAIR * HEAD_DIM
    return pl.pallas_call(
        _attn_diff_kernel,
        out_shape=jax.ShapeDtypeStruct((bsz, seq, GROUP_WIDTH), BF16),
        grid=(bsz, N_HEADS // DIFF_PAIR, seq // ATT_TQ),
        in_specs=[
            pl.BlockSpec((1, ATT_TQ, qw), lambda b, p, i: (b, i, p)),
            pl.BlockSpec((1, seq, qw), lambda b, p, i: (b, 0, p)),
            pl.BlockSpec((1, seq, vw), lambda b, p, i: (b, 0, p)),
            _layer_spec(lam.shape, layer),
            _layer_spec(subln_g.shape, layer),
            pl.BlockSpec((DIFF_PAIR, ATT_TQ, ATT_TK), lambda b, p, i: (p, 0, 0)),
            _const_spec(sel.shape),
        ],
        out_specs=pl.BlockSpec((1, ATT_TQ, ow), lambda b, p, i: (b, i, p)),
        compiler_params=_params(),
        name="attn_diff",
    )(q, k, v, lam, subln_g, bias, sel)


def _attn_dil_kernel(q_ref, k_ref, v_ref, bias_ref, o_ref):
    tq = q_ref.shape[1]
    nk = k_ref.shape[1] // ATT_TK
    reach = DIL_BAND // ATT_TK
    qi = pl.program_id(1)
    n_off = 2 * reach + 1
    carry = tuple(_softmax_init(tq) for _ in range(N_HEADS))
    for t in [reach] + [t for t in range(n_off) if t != reach]:
        j = qi + (t - reach)
        valid = jnp.logical_and(j >= 0, j < nk)
        off = pl.multiple_of(jnp.clip(j, 0, nk - 1) * ATT_TK, ATT_TK)
        tb = jnp.where(valid, t, n_off)
        scores = [lax.dot_general(q_ref[0, :, LANES * h:LANES * (h + 1)], _chunk(k_ref, off, h), _NT,
                                  preferred_element_type=F32) + bias_ref[h, tb] for h in range(N_HEADS)]
        vs = [_chunk(v_ref, off, h) for h in range(N_HEADS)]
        carry = _softmax_steps(scores, vs, carry)
    o_ref[0] = jnp.concatenate([_normalize(acc) for _, acc in carry], axis=1).astype(o_ref.dtype)


def _attn_dil_call(q, k, v, bias):
    bsz, seq, qw = q.shape
    return pl.pallas_call(
        _attn_dil_kernel,
        out_shape=jax.ShapeDtypeStruct((bsz, seq, GROUP_WIDTH), BF16),
        grid=(bsz, seq // ATT_TQ),
        in_specs=[
            pl.BlockSpec((1, ATT_TQ, qw), lambda b, i: (b, i, 0)),
            pl.BlockSpec((1, seq, qw), lambda b, i: (b, 0, 0)),
            pl.BlockSpec((1, seq, qw), lambda b, i: (b, 0, 0)),
            _const_spec(bias.shape),
        ],
        out_specs=pl.BlockSpec((1, ATT_TQ, GROUP_WIDTH), lambda b, i: (b, i, 0)),
        compiler_params=_params(),
        name="attn_dilated",
    )(q, k, v, bias)


def _layer_norm(z, g, b):
    mu = jnp.mean(z, axis=-1, keepdims=True)
    zc = z - mu
    var = jnp.mean(zc * zc, axis=-1, keepdims=True)
    return zc * lax.rsqrt(var + LN_EPS) * g + b


def _outproj_kernel(ya_ref, yb_ref, yc_ref, yd_ref, x_ref, mod_ref, wo_ref, ln_ref, o_ref, *, alpha):
    half = x_ref.shape[1] // 2
    for r in (0, half):
        rows = slice(r, r + half)
        ycat = jnp.concatenate([ya_ref[0, rows], yb_ref[0, rows], yc_ref[0, rows], yd_ref[0, rows]], axis=1)
        y = jnp.dot(ycat, wo_ref[...], preferred_element_type=F32)
        z = alpha * x_ref[0, rows] + mod_ref[0, 2:3, :] * y
        o_ref[0, rows] = _layer_norm(z, ln_ref[0:1, :], ln_ref[1:2, :])


def _outproj_call(ys, x, mod, layer, wo, ln, *, alpha):
    bsz, seq, d = x.shape
    tm = ROW_TILE
    yspec = pl.BlockSpec((1, tm, GROUP_WIDTH), lambda b, i: (b, i, 0))
    xspec = pl.BlockSpec((1, tm, d), lambda b, i: (b, i, 0))
    return pl.pallas_call(
        functools.partial(_outproj_kernel, alpha=alpha),
        out_shape=jax.ShapeDtypeStruct((bsz, seq, d), F32),
        grid=(bsz, seq // tm),
        in_specs=[yspec, yspec, yspec, yspec, xspec,
                  _mod_spec(d, layer), _layer_spec(wo.shape, layer), _layer_spec(ln.shape, layer)],
        out_specs=xspec,
        compiler_params=_params(),
        name="out_proj_ln",
    )(*ys, x, mod, wo, ln)


def _mlp_kernel(x_ref, mod_ref, wup_ref, wdn_ref, ln_ref, o_ref, *, alpha):
    x = x_ref[0]
    h = (x * (1.0 + mod_ref[0, 4:5, :]) + mod_ref[0, 3:4, :]).astype(BF16)
    u = jnp.zeros(x.shape, F32)
    for c in range(MLP_HIDDEN // MLP_CHUNK):
        a = jnp.dot(h, wup_ref[:, c * MLP_CHUNK:(c + 1) * MLP_CHUNK], preferred_element_type=F32)
        a = jnp.maximum(a, 0.0)
        u = u + jnp.dot((a * a).astype(BF16), wdn_ref[c * MLP_CHUNK:(c + 1) * MLP_CHUNK, :],
                        preferred_element_type=F32)
    z = alpha * x + mod_ref[0, 5:6, :] * u
    o_ref[0] = _layer_norm(z, ln_ref[0:1, :], ln_ref[1:2, :])


def _mlp_call(x, mod, layer, wup, wdn, ln, *, alpha):
    bsz, seq, d = x.shape
    tm = ROW_TILE
    xspec = pl.BlockSpec((1, tm, d), lambda b, i: (b, i, 0))
    return pl.pallas_call(
        functools.partial(_mlp_kernel, alpha=alpha),
        out_shape=jax.ShapeDtypeStruct((bsz, seq, d), F32),
        grid=(bsz, seq // tm),
        in_specs=[xspec, _mod_spec(d, layer), _layer_spec(wup.shape, layer), _layer_spec(wdn.shape, layer),
                  _layer_spec(ln.shape, layer)],
        out_specs=xspec,
        compiler_params=_params(),
        name="mlp_ln",
    )(x, mod, wup, wdn, ln)


def _rot_half_source(width, block):
    half = block // 2
    d = np.arange(width)
    inner = d % block
    src = np.where(inner < half, d + half, d - half)
    sign = np.where(inner < half, -1.0, 1.0)
    return src, sign.astype(np.float32)


def _placement(n_in, n_out, pairs):
    m = np.zeros((n_in, n_out), np.float32)
    for r, c in pairs:
        m[r, c] = 1.0
    return jnp.asarray(m, BF16)


def _static_consts():
    pa = _placement(256, 1024, [(32 * hc + d, LANES * hc + d) for hc in range(8) for d in range(32)])
    pcq = _placement(256, 512, [(64 * h + d, LANES * h + d) for h in range(4) for d in range(64)])
    pck = _placement(128, 256, [(64 * g + d, LANES * g + d) for g in range(2) for d in range(64)])
    e = _placement(128, 512, [(s * 32 + d, LANES * h + MLA_NOPE_DIM + d)
                              for h in range(4) for s in range(2) for d in range(32)])
    g = np.zeros((256, 256), np.float32)
    for h in range(4):
        g[64 * h:64 * (h + 1), 64 * h:64 * (h + 1)] = 1.0 / HEAD_DIM
    sel = np.zeros((2, LANES, LANES), np.float32)
    sel[0, HEAD_DIM, :] = 1.0
    sel[1, 0:HEAD_DIM, 0:HEAD_DIM] = 1.0 / HEAD_DIM
    return dict(pa=pa, pcq=pcq, pck=pck, e=e, g=jnp.asarray(g, BF16), sel=jnp.asarray(sel, BF16))


def _rope_tables(pos, half):
    freqs = ROPE_THETA ** (-jnp.arange(half, dtype=F32) / half)
    ang = pos[:, None] * freqs[None, :]
    cos = jnp.concatenate([jnp.cos(ang), jnp.cos(ang)], axis=1)
    sin = jnp.concatenate([jnp.sin(ang), jnp.sin(ang)], axis=1)
    return cos, sin


def _position_tables(seq, slopes_a, slopes_d):
    pos = jnp.arange(seq, dtype=F32)
    row = jnp.floor(pos / GRID_W)
    col = pos - row * GRID_W
    cos32, sin32 = _rope_tables(pos, MLA_ROPE_DIM // 2)
    zeros32 = jnp.zeros((seq, 32), F32)
    ones64 = jnp.ones((seq, 64), F32)
    zeros64 = jnp.zeros((seq, 64), F32)
    b_scale = MLA_QK_DIM ** -0.5 * LOG2E
    cos_b = jnp.tile(jnp.concatenate([ones64, cos32, zeros32], 1), (1, 4)) * b_scale
    sin_b = jnp.tile(jnp.concatenate([zeros64, sin32, zeros32], 1), (1, 4)) * b_scale
    tkr = jnp.concatenate([cos32, sin32, zeros64], 1)
    cr, sr = _rope_tables(row, HEAD_DIM // 4)
    cc, sc = _rope_tables(col, HEAD_DIM // 4)
    cos_h = jnp.concatenate([cr, cc], 1)
    sin_h = jnp.concatenate([sr, sc], 1)
    c_scale = HEAD_DIM ** -0.5 * LOG2E
    tc = jnp.stack([jnp.tile(cos_h, (1, 4)), jnp.tile(sin_h, (1, 4))]) * c_scale
    tck = jnp.stack([jnp.tile(cos_h, (1, 2)), jnp.tile(sin_h, (1, 2))])
    ipos = np.arange(seq)
    hi = (ipos // POS_SPLIT).astype(np.float32)
    lo = (ipos % POS_SPLIT).astype(np.float32)
    qaug = np.zeros((seq, 8 * LANES), np.float32)
    kaug = np.zeros((seq, 8 * LANES), np.float32)
    terms, rest = [], LOG2E
    for _ in range(LOG2E_TERMS):
        t = float(np.float32(rest).astype(ml_dtypes.bfloat16))
        terms.append(t)
        rest -= t
    for hc in range(8):
        s = float(slopes_a[hc // 2])
        for n, t in enumerate(terms):
            base = LANES * hc + AUG_LANE + 4 * n
            qaug[:, base + 0] = hi
            qaug[:, base + 1] = lo
            qaug[:, base + 2] = s * t * POS_SPLIT
            qaug[:, base + 3] = s * t
            kaug[:, base + 0] = -s * t * POS_SPLIT
            kaug[:, base + 1] = -s * t
            kaug[:, base + 2] = hi
            kaug[:, base + 3] = lo
    reach = DIL_BAND // ATT_TK
    a = np.arange(ATT_TQ)[:, None]
    b = np.arange(ATT_TK)[None, :]
    tabs = []
    for t in range(2 * reach + 1):
        o = (t - reach) * ATT_TK + b - a
        cnt = np.zeros(o.shape, np.float32)
        for w, dil in DIL_PAIRS:
            cnt += ((np.abs(o) <= w // 2) & (o % dil == 0)).astype(np.float32)
        tabs.append((np.abs(o).astype(np.float32), cnt))
    dist = np.stack([t[0] for t in tabs])
    cnt = np.stack([t[1] for t in tabs])
    logc = np.where(cnt > 0, np.log(np.maximum(cnt, 1.0)), NEG_INF).astype(np.float32)
    bias_d = np.stack([np.where(cnt > 0, (-float(s) * dist + logc) * LOG2E, NEG_INF) for s in slopes_d])
    bias_d = np.concatenate([bias_d, np.full_like(bias_d[:, :1], NEG_INF)], axis=1)
    bias_a = np.stack([(-float(s) * LOG2E * np.abs(a - b)).astype(np.float32) for s in slopes_a])
    return dict(tb=jnp.stack([cos_b, sin_b]), tc=tc, tck=tck, tkr=tkr, bias_a=jnp.asarray(bias_a),
                qaug=jnp.asarray(qaug, BF16), kaug=jnp.asarray(kaug, BF16),
                bias_d=jnp.asarray(bias_d.astype(np.float32)))


class _ColumnPlan:
    def __init__(self):
        self.idx, self.sgn = [], []

    def plain(self, start, width):
        self.idx.extend(range(start, start + width))
        self.sgn.extend([1.0] * width)

    def zeros(self, width):
        self.idx.extend([0] * width)
        self.sgn.extend([0.0] * width)

    def take(self, indices):
        self.idx.extend(np.asarray(indices).tolist())
        self.sgn.extend([1.0] * len(indices))

    def partner(self, start, width, block):
        src, sign = _rot_half_source(width, block)
        self.idx.extend((start + src).tolist())
        self.sgn.extend(sign.tolist())

    def placed(self, start, n_heads, stride, width):
        for h in range(n_heads):
            self.plain(start + h * stride, width)
            self.zeros(LANES - width)

    def gather(self, w):
        cols = jnp.take(w, np.asarray(self.idx, np.int32), axis=-1)
        return cols * np.asarray(self.sgn, np.float32)


VEC_WIDTH = 512


def _stacked_weights(w_in, mla_w_uq, mla_w_ukv, mla_q_norm_g, mla_kv_norm_g, gqa_q_norm_g, gqa_k_norm_g):
    o_aq, o_ak, o_av, o_cq, o_ckv, o_kr = 0, 256, 512, 768, 1152, 1408
    o_gq, o_gk, o_gv, o_dq, o_dk, o_dv = 1440, 1696, 1824, 1952, 2208, 2464
    p = _ColumnPlan()
    p.plain(o_aq, 256); p.plain(o_ak, 256); p.plain(o_cq, 384); p.plain(o_ckv, 256)
    p.plain(o_kr, 32); p.partner(o_kr, 32, 32); p.zeros(64)
    p.plain(o_gq, 256); p.partner(o_gq, 256, 32); p.plain(o_gk, 128); p.partner(o_gk, 128, 32)
    p.plain(o_dq, 256); p.plain(o_dk, 256)
    p.placed(o_av, 4, 64, 64); p.placed(o_gv, 2, 64, 64); p.placed(o_dv, 4, 64, 64)
    assert len(p.idx) == W1_COLS
    w1 = p.gather(w_in).astype(BF16)
    p = _ColumnPlan()
    p.placed(0, 4, MLA_QK_DIM, MLA_QK_DIM)
    for h in range(4):
        p.zeros(MLA_NOPE_DIM)
        p.partner(h * MLA_QK_DIM + MLA_NOPE_DIM, MLA_ROPE_DIM, MLA_ROPE_DIM)
        p.zeros(LANES - MLA_QK_DIM)
    wuq = p.gather(mla_w_uq).astype(BF16)
    p = _ColumnPlan()
    p.placed(0, 4, MLA_NOPE_DIM + MLA_V_DIM, MLA_NOPE_DIM)
    p.placed(MLA_NOPE_DIM, 4, MLA_NOPE_DIM + MLA_V_DIM, MLA_V_DIM)
    wukv = p.gather(mla_w_ukv).astype(BF16)
    n_layers = w_in.shape[0]
    gains = jnp.concatenate([jnp.ones((n_layers, 1), F32), mla_q_norm_g.astype(F32), mla_kv_norm_g.astype(F32),
                             gqa_q_norm_g.astype(F32), gqa_k_norm_g.astype(F32)], axis=1)
    o_qn, o_kvn = 1, 1 + MLA_Q_RANK
    o_gqg, o_gkg = o_kvn + MLA_KV_RANK, o_kvn + MLA_KV_RANK + HEAD_DIM
    src64, _ = _rot_half_source(HEAD_DIM, HEAD_DIM // 2)
    p = _ColumnPlan()
    rows = (
        [("zeros", HEAD_DIM), ("plain", 0, 1), ("zeros", LANES - HEAD_DIM - 1)] * 4,
        [("plain", o_qn, MLA_Q_RANK)],
        [("plain", o_kvn, MLA_KV_RANK)],
        [("plain", o_gqg, HEAD_DIM)] * 4,
        [("take", o_gqg + src64)] * 4,
        [("plain", o_gkg, HEAD_DIM)] * 2,
        [("take", o_gkg + src64)] * 2,
        [("zeros", VEC_WIDTH)],
    )
    for entries in rows:
        for kind, *args in entries:
            getattr(p, kind)(*args)
        p.zeros(-len(p.idx) % VEC_WIDTH)
    vec = p.gather(gains).reshape(n_layers, len(rows), VEC_WIDTH)
    return dict(w1=w1, wuq=wuq, wukv=wukv, vec=vec)


def kernel(x, c, w_ada, b_ada, w_in, w_o, diff_lambda, diff_subln_g, mla_q_norm_g, mla_w_uq, mla_kv_norm_g, mla_w_ukv, gqa_q_norm_g, gqa_k_norm_g, ln_attn_g, ln_attn_b, w_up, w_down, ln_mlp_g, ln_mlp_b):
    alpha = (2 * DEPTH) ** 0.25
    seq = x.shape[1]
    n_slopes = 2 * N_HEADS
    slopes = [2.0 ** (-8.0 * (n + 1.0) / n_slopes) for n in range(n_slopes)]
    slopes_a, slopes_d = tuple(slopes[0::2]), tuple(slopes[1::2])
    consts = _static_consts()
    tables = _position_tables(seq, slopes_a, slopes_d)
    mod = _ada_call(c, w_ada, b_ada)
    layer_w = _stacked_weights(w_in, mla_w_uq, mla_w_ukv, mla_q_norm_g, mla_kv_norm_g,
                               gqa_q_norm_g, gqa_k_norm_g)
    lambda_init = np.asarray([0.8 - 0.6 * math.exp(-0.3 * l) for l in range(DEPTH)], np.float32)
    lam_pack = jnp.concatenate(
        [diff_lambda.astype(F32), jnp.broadcast_to(lambda_init[:, None, None], (DEPTH, 1, DIFF_QK_DIM))], axis=1)
    subln_g = jnp.pad(diff_subln_g.astype(F32), ((0, 0), (0, LANES - HEAD_DIM)))[:, None, :]
    wo, wup, wdn = w_o.astype(BF16), w_up.astype(BF16), w_down.astype(BF16)
    ln_attn = jnp.stack([ln_attn_g, ln_attn_b], axis=1)
    ln_mlp = jnp.stack([ln_mlp_g, ln_mlp_b], axis=1)
    for l in range(DEPTH):
        (qa, ka, va, qb, kb, vb, qc, kc, vc, qd, kd, vd) = _proj_call(x, mod, l, consts, layer_w, tables)
        ya = _attn_diff_call(qa, ka, va, l, lam_pack, subln_g, tables["bias_a"], consts["sel"])
        yb = _attn_plain_call(qb, kb, vb, n_heads=N_HEADS, q_per_kv=1)
        yc = _attn_plain_call(qc, kc, vc, n_heads=N_HEADS, q_per_kv=N_HEADS // GQA_KV_HEADS)
        yd = _attn_dil_call(qd, kd, vd, tables["bias_d"])
        x = _outproj_call((ya, yb, yc, yd), x, mod, l, wo, ln_attn, alpha=alpha)
        x = _mlp_call(x, mod, l, wup, wdn, ln_mlp, alpha=alpha)
    return x
```

```python
import functools
import math

import jax
import jax.numpy as jnp
import ml_dtypes
import numpy as np
from jax import lax
from jax.experimental import pallas as pl
from jax.experimental.pallas import tpu as pltpu

D_MODEL = 1024
DEPTH = 2
HEAD_DIM = 64
GROUP_WIDTH = D_MODEL // 4
N_HEADS = GROUP_WIDTH // HEAD_DIM
DIFF_QK_DIM = HEAD_DIM // 2
MLA_NOPE_DIM = HEAD_DIM
MLA_ROPE_DIM = HEAD_DIM // 2
MLA_V_DIM = HEAD_DIM
MLA_QK_DIM = MLA_NOPE_DIM + MLA_ROPE_DIM
MLA_Q_RANK = N_HEADS * MLA_QK_DIM
MLA_KV_RANK = 4 * MLA_V_DIM
GQA_KV_HEADS = N_HEADS // 2
GRID_W = 64
DIL_PAIRS = ((128, 1), (512, 4), (2048, 16))
MLP_HIDDEN = 4 * D_MODEL
ROPE_THETA = 10000.0
RMS_EPS = 1e-6
LN_EPS = 1e-5
NEG_INF = -1e30
LOG2E = math.log2(math.e)

LANES = 128
VMEM_LIMIT_BYTES = 56 * 1024 * 1024

PROJ_ROWS = 512
ATT_TQ = 512
ATT_TK = 512
ROW_TILE = 512
MLP_CHUNK = 1024
AUG_LANE = DIFF_QK_DIM
LOG2E_TERMS = 5
POS_SPLIT = 64
DIFF_PAIR = 2
DIL_FAR_STRIDE = 16
DIL_NEAR = 256
FAR_GROUP = 4
M_LANE = HEAD_DIM + 1

_NT = (((1,), (1,)), ((), ()))

F32 = jnp.float32
BF16 = jnp.bfloat16


def _params(**kw):
    return pltpu.CompilerParams(vmem_limit_bytes=VMEM_LIMIT_BYTES, **kw)


def _const_spec(shape):
    nd = len(shape)
    return pl.BlockSpec(shape, lambda *_: (0,) * nd, pipeline_mode=pl.Buffered(1))


def _layer_spec(shape, layer):
    nd = len(shape)
    return pl.BlockSpec((None,) + tuple(shape[1:]), lambda *_: (layer,) + (0,) * (nd - 1),
                        pipeline_mode=pl.Buffered(1))


def _mod_spec(d, layer):
    return pl.BlockSpec((None, 1, 6, d), lambda b, *_: (layer, b, 0, 0))


def _ada_kernel(c_ref, w_ref, b_ref, o_ref):
    c = c_ref[...]
    cond = c * (1.0 / (1.0 + jnp.exp(-c)))
    o_ref[0, 0] = jnp.dot(cond, w_ref[0], preferred_element_type=F32) + b_ref[0, 0]


def _ada_call(c, w_ada, b_ada):
    n_layers, d, _ = w_ada.shape
    bsz = c.shape[0]
    b4 = b_ada.reshape(n_layers, 6, 1, d)
    out = pl.pallas_call(
        _ada_kernel,
        out_shape=jax.ShapeDtypeStruct((n_layers, 6, bsz, d), F32),
        grid=(n_layers, 6),
        in_specs=[
            pl.BlockSpec((bsz, d), lambda l, j: (0, 0)),
            pl.BlockSpec((1, d, d), lambda l, j: (l, 0, j)),
            pl.BlockSpec((1, 1, 1, d), lambda l, j: (l, j, 0, 0)),
        ],
        out_specs=pl.BlockSpec((1, 1, bsz, d), lambda l, j: (l, j, 0, 0)),
        compiler_params=_params(),
        name="adaln_mod",
    )(c, w_ada, b4)
    return jnp.transpose(out, (0, 2, 1, 3))


_SEG_WIDTHS = (
    ("a_q", 256), ("a_k", 256), ("b_cq", 384), ("b_ckv", 256), ("b_kr", 128),
    ("c_q", 256), ("c_q_sw", 256), ("c_k", 128), ("c_k_sw", 128),
    ("d_q", 256), ("d_k", 256), ("v_a", 512), ("v_c", 256), ("v_d", 512),
)
_SEG = {}
_off = 0
for _name, _w in _SEG_WIDTHS:
    _SEG[_name] = (_off, _off + _w)
    _off += _w
W1_COLS = _off


def _lane_spread(x, mat):
    hi = x.astype(BF16)
    lo = (x - hi.astype(F32)).astype(BF16)
    return (jnp.dot(hi, mat, preferred_element_type=F32)
            + jnp.dot(lo, mat, preferred_element_type=F32))


def _rms_rows(x, gsum):
    return lax.rsqrt(_lane_spread(x * x, gsum) + RMS_EPS)


def _proj_kernel(x_ref, mod_ref, w1_ref, pa_ref, wuq_ref, wukv_ref, e_ref, g_ref, pcq_ref, pck_ref,
                 vec_ref, tb_ref, tc_ref, tck_ref, tkr_ref, qaug_ref, kaug_ref,
                 qa_ref, ka_ref, va_ref, qb_ref, kb_ref, vb_ref, qc_ref, kc_ref, vc_ref,
                 qd_ref, kd_ref, vd_ref):
    sh = mod_ref[0, 0:1, :]
    sc = mod_ref[0, 1:2, :]
    h = (x_ref[0] * (1.0 + sc) + sh).astype(BF16)

    def seg(name):
        a, b = _SEG[name]
        return jnp.dot(h, w1_ref[:, a:b], preferred_element_type=F32)

    def vec(row, width):
        return vec_ref[row:row + 1, 0:width]

    a_scale = DIFF_QK_DIM ** -0.5 * LOG2E
    qa = jnp.dot((seg("a_q") * a_scale).astype(BF16), pa_ref[...], preferred_element_type=F32)
    qa_ref[0] = qa.astype(BF16) + qaug_ref[...]
    ka = jnp.dot(seg("a_k").astype(BF16), pa_ref[...], preferred_element_type=F32)
    ka_ref[0] = ka.astype(BF16) + kaug_ref[...]
    va_ref[0] = (seg("v_a") + vec(0, 512)).astype(BF16)

    cq = seg("b_cq")
    r = lax.rsqrt(jnp.mean(cq * cq, axis=-1, keepdims=True) + RMS_EPS)
    cqn = (cq * r * vec(1, 384)).astype(BF16)
    q2 = jnp.dot(cqn, wuq_ref[...], preferred_element_type=F32)
    qb_ref[0] = (q2[:, 0:512] * tb_ref[0] + q2[:, 512:1024] * tb_ref[1]).astype(BF16)
    ckv = seg("b_ckv")
    r = lax.rsqrt(jnp.mean(ckv * ckv, axis=-1, keepdims=True) + RMS_EPS)
    ckvn = (ckv * r * vec(2, 256)).astype(BF16)
    kv2 = jnp.dot(ckvn, wukv_ref[...], preferred_element_type=F32)
    kr = (seg("b_kr") * tkr_ref[...]).astype(BF16)
    kb_ref[0] = (kv2[:, 0:512] + jnp.dot(kr, e_ref[...], preferred_element_type=F32)).astype(BF16)
    vb_ref[0] = (kv2[:, 512:1024] + vec(0, 512)).astype(BF16)

    cqx = seg("c_q")
    r = _rms_rows(cqx, g_ref[...])
    qc = cqx * r * vec(3, 256) * tc_ref[0] + seg("c_q_sw") * r * vec(4, 256) * tc_ref[1]
    qc_ref[0] = jnp.dot(qc.astype(BF16), pcq_ref[...], preferred_element_type=F32).astype(BF16)
    ckx = seg("c_k")
    r = _rms_rows(ckx, g_ref[0:128, 0:128])
    kc = ckx * r * vec(5, 128) * tck_ref[0] + seg("c_k_sw") * r * vec(6, 128) * tck_ref[1]
    kc_ref[0] = jnp.dot(kc.astype(BF16), pck_ref[...], preferred_element_type=F32).astype(BF16)
    vc_ref[0] = (seg("v_c") + vec(0, 256)).astype(BF16)

    d_scale = HEAD_DIM ** -0.5 * LOG2E
    qd = jnp.dot((seg("d_q") * d_scale).astype(BF16), pcq_ref[...], preferred_element_type=F32)
    qd_ref[0] = qd.astype(BF16)
    kd_ref[0] = jnp.dot(seg("d_k").astype(BF16), pcq_ref[...], preferred_element_type=F32).astype(BF16)
    vd_ref[0] = (seg("v_d") + vec(0, 512)).astype(BF16)


def _proj_call(x, mod, layer, consts, layer_w, tables):
    w1 = layer_w["w1"]
    bsz, seq, d = x.shape
    tm = PROJ_ROWS
    nt = seq // tm
    out_widths = (1024, 1024, 512, 512, 512, 512, 512, 256, 256, 512, 512, 512)
    row_spec = lambda w: pl.BlockSpec((1, tm, w), lambda b, i: (b, i, 0))
    tab3 = lambda w: pl.BlockSpec((2, tm, w), lambda b, i: (0, i, 0))
    tab2 = lambda w: pl.BlockSpec((tm, w), lambda b, i: (i, 0))
    in_specs = [
        row_spec(d),
        _mod_spec(d, layer),
        _layer_spec(w1.shape, layer),
        _const_spec(consts["pa"].shape),
        _layer_spec(layer_w["wuq"].shape, layer),
        _layer_spec(layer_w["wukv"].shape, layer),
        _const_spec(consts["e"].shape),
        _const_spec(consts["g"].shape),
        _const_spec(consts["pcq"].shape),
        _const_spec(consts["pck"].shape),
        _layer_spec(layer_w["vec"].shape, layer),
        tab3(512), tab3(256), tab3(128), tab2(128), tab2(1024), tab2(1024),
    ]
    return pl.pallas_call(
        _proj_kernel,
        out_shape=tuple(jax.ShapeDtypeStruct((bsz, seq, w), BF16) for w in out_widths),
        grid=(bsz, nt),
        in_specs=in_specs,
        out_specs=tuple(row_spec(w) for w in out_widths),
        compiler_params=_params(),
        name="proj_prep",
    )(x, mod, w1, consts["pa"], layer_w["wuq"], layer_w["wukv"], consts["e"], consts["g"],
      consts["pcq"], consts["pck"], layer_w["vec"], tables["tb"], tables["tc"], tables["tck"],
      tables["tkr"], tables["qaug"], tables["kaug"])


def _softmax_steps(scores, vs, carry):
    stats = []
    for s, (m, _) in zip(scores, carry):
        m_new = jnp.maximum(m, jnp.max(s, axis=-1, keepdims=True))
        if s.shape[1] % LANES == 0:
            shift = pltpu.repeat(m_new, s.shape[1] // LANES, axis=1)
        else:
            shift = m_new[:, 0:1]
        stats.append((m_new, jnp.exp2(s - shift).astype(BF16), jnp.exp2(m - m_new)))
    out = []
    for (m_new, p, alpha), v, (_, acc) in zip(stats, vs, carry):
        out.append((m_new, alpha * acc + jnp.dot(p, v, preferred_element_type=F32)))
    return tuple(out)


def _softmax_init(tq):
    return jnp.full((tq, LANES), NEG_INF, F32), jnp.zeros((tq, LANES), F32)


def _normalize(acc):
    return acc[:, 0:HEAD_DIM] / acc[:, HEAD_DIM:HEAD_DIM + 1]


def _chunk(ref, off, group):
    return ref[0, pl.ds(off, ATT_TK), LANES * group:LANES * (group + 1)]


def _attn_plain_kernel(q_ref, k_ref, v_ref, o_ref, *, n_heads, q_per_kv):
    tq = q_ref.shape[1]
    nk = k_ref.shape[1] // ATT_TK
    carry = tuple(_softmax_init(tq) for _ in range(n_heads))
    for j in range(nk):
        off = j * ATT_TK
        scores = [lax.dot_general(q_ref[0, :, LANES * h:LANES * (h + 1)], _chunk(k_ref, off, h // q_per_kv),
                                  _NT, preferred_element_type=F32) for h in range(n_heads)]
        vs = [_chunk(v_ref, off, h // q_per_kv) for h in range(n_heads)]
        carry = _softmax_steps(scores, vs, carry)
    o_ref[0] = jnp.concatenate([_normalize(acc) for _, acc in carry], axis=1).astype(o_ref.dtype)


def _attn_plain_call(q, k, v, *, n_heads, q_per_kv):
    bsz, seq, qw = q.shape
    kw = k.shape[2]
    kern = functools.partial(_attn_plain_kernel, n_heads=n_heads, q_per_kv=q_per_kv)
    return pl.pallas_call(
        kern,
        out_shape=jax.ShapeDtypeStruct((bsz, seq, n_heads * HEAD_DIM), BF16),
        grid=(bsz, seq // ATT_TQ),
        in_specs=[
            pl.BlockSpec((1, ATT_TQ, qw), lambda b, i: (b, i, 0)),
            pl.BlockSpec((1, seq, kw), lambda b, i: (b, 0, 0)),
            pl.BlockSpec((1, seq, kw), lambda b, i: (b, 0, 0)),
        ],
        out_specs=pl.BlockSpec((1, ATT_TQ, n_heads * HEAD_DIM), lambda b, i: (b, i, 0)),
        compiler_params=_params(),
        name="attn_plain",
    )(q, k, v)


def _attn_diff_kernel(q_ref, k_ref, v_ref, lam_ref, g_ref, bias_ref, sel_ref, o_ref):
    tq = q_ref.shape[1]
    nk = k_ref.shape[1] // ATT_TK
    qi = pl.program_id(2)
    n_comp = 2 * DIFF_PAIR
    lane = lax.broadcasted_iota(jnp.int32, (1, LANES), 1)

    def step(off, carry, aug_sign, diagonal):
        aug = jnp.where(lane < AUG_LANE, 1.0, aug_sign).astype(BF16)
        scores = []
        for hc in range(n_comp):
            q = q_ref[0, :, LANES * hc:LANES * (hc + 1)] * aug
            s = lax.dot_general(q, _chunk(k_ref, off, hc), _NT, preferred_element_type=F32)
            if diagonal:
                s = s + bias_ref[hc // 2]
            scores.append(s)
        vs = [_chunk(v_ref, off, hc // 2) for hc in range(n_comp)]
        return _softmax_steps(scores, vs, carry)

    carry = step(pl.multiple_of(qi * ATT_TK, ATT_TK), tuple(_softmax_init(tq) for _ in range(n_comp)),
                 0.0, True)
    for j in range(nk - 1):
        jj = j + (j >= qi).astype(jnp.int32)
        sign = jnp.where(jj < qi, 1.0, -1.0)
        carry = step(pl.multiple_of(jj * ATT_TK, ATT_TK), carry, sign, False)

    lf = lam_ref[...]
    lambda_init = lf[4:5, 0:1]
    lam = (jnp.exp(jnp.sum(lf[0:1] * lf[1:2], axis=-1, keepdims=True))
           - jnp.exp(jnp.sum(lf[2:3] * lf[3:4], axis=-1, keepdims=True)) + lambda_init)
    outs = []
    for h in range(DIFF_PAIR):
        a0, a1 = carry[2 * h][1], carry[2 * h + 1][1]
        y = a0 / _lane_spread(a0, sel_ref[0]) - lam * (a1 / _lane_spread(a1, sel_ref[0]))
        r = lax.rsqrt(_lane_spread(y * y, sel_ref[1]) + RMS_EPS)
        outs.append((y * r * g_ref[...] * (1.0 - lambda_init))[:, 0:HEAD_DIM])
    o_ref[0] = jnp.concatenate(outs, axis=1).astype(o_ref.dtype)


def _attn_diff_call(q, k, v, layer, lam, subln_g, bias, sel):
    bsz, seq, _ = q.shape
    qw = 2 * DIFF_PAIR * LANES
    vw = DIFF_PAIR * LANES
    ow = DIFF_PAIR * HEAD_DIM
    return pl.pallas_call(
        _attn_diff_kernel,
        out_shape=jax.ShapeDtypeStruct((bsz, seq, GROUP_WIDTH), BF16),
        grid=(bsz, N_HEADS // DIFF_PAIR, seq // ATT_TQ),
        in_specs=[
            pl.BlockSpec((1, ATT_TQ, qw), lambda b, p, i: (b, i, p)),
            pl.BlockSpec((1, seq, qw), lambda b, p, i: (b, 0, p)),
            pl.BlockSpec((1, seq, vw), lambda b, p, i: (b, 0, p)),
            _layer_spec(lam.shape, layer),
            _layer_spec(subln_g.shape, layer),
            pl.BlockSpec((DIFF_PAIR, ATT_TQ, ATT_TK), lambda b, p, i: (p, 0, 0)),
            _const_spec(sel.shape),
        ],
        out_specs=pl.BlockSpec((1, ATT_TQ, ow), lambda b, p, i: (b, i, p)),
        compiler_params=_params(),
        name="attn_diff",
    )(q, k, v, lam, subln_g, bias, sel)


def _attn_dil_far_kernel(q_ref, k_ref, v_ref, bias_ref, st_ref):
    n = q_ref.shape[1]
    lane = lax.broadcasted_iota(jnp.int32, (1, LANES), 1)
    for c in range(FAR_GROUP):
        heads = [slice(LANES * (N_HEADS * c + h), LANES * (N_HEADS * c + h + 1)) for h in range(N_HEADS)]
        scores = [lax.dot_general(q_ref[0, :, hs], k_ref[0, :, hs], _NT, preferred_element_type=F32) + bias_ref[h]
                  for h, hs in enumerate(heads)]
        carry = _softmax_steps(scores, [v_ref[0, :, hs] for hs in heads], tuple(_softmax_init(n) for _ in heads))
        for hs, (m, acc) in zip(heads, carry):
            st_ref[0, :, hs] = jnp.where(lane == M_LANE, m, acc)


def _attn_dil_far_call(q, k, v, bias):
    bsz, seq, qw = q.shape
    n = seq // DIL_FAR_STRIDE
    view = lambda t: t.reshape(bsz, n, DIL_FAR_STRIDE * qw)
    spec = pl.BlockSpec((1, n, FAR_GROUP * qw), lambda b, r: (b, 0, r))
    st = pl.pallas_call(
        _attn_dil_far_kernel,
        out_shape=jax.ShapeDtypeStruct((bsz, n, DIL_FAR_STRIDE * qw), F32),
        grid=(bsz, DIL_FAR_STRIDE // FAR_GROUP),
        in_specs=[spec, spec, spec, _const_spec(bias.shape)],
        out_specs=spec,
        compiler_params=_params(),
        name="attn_dilated_far",
    )(view(q), view(k), view(v), bias)
    return st.reshape(bsz, seq, qw)


def _attn_dil_kernel(q_ref, k_ref, v_ref, st_ref, bias_ref, side_ref, o_ref):
    tq, seq = q_ref.shape[1], k_ref.shape[1]
    q0 = pl.program_id(1) * tq
    heads = [slice(LANES * h, LANES * (h + 1)) for h in range(N_HEADS)]
    carry = tuple((jnp.broadcast_to(st_ref[0, :, LANES * h + M_LANE:LANES * h + M_LANE + 1], (tq, LANES)),
                   st_ref[0, :, hs]) for h, hs in enumerate(heads))

    def step(carry, off, width, bias_of_head):
        scores = [lax.dot_general(q_ref[0, :, hs], k_ref[0, pl.ds(off, width), hs], _NT,
                                  preferred_element_type=F32) + bias_of_head(h) for h, hs in enumerate(heads)]
        return _softmax_steps(scores, [v_ref[0, pl.ds(off, width), hs] for hs in heads], carry)

    carry = step(carry, pl.multiple_of(q0, tq), tq, lambda h: bias_ref[h])
    left = jnp.where(q0 > 0, 0, 2)
    right = jnp.where(q0 + tq < seq, 1, 2)
    carry = step(carry, pl.multiple_of(jnp.maximum(q0 - DIL_NEAR, 0), DIL_NEAR), DIL_NEAR,
                 lambda h: side_ref[h, left])
    carry = step(carry, pl.multiple_of(jnp.minimum(q0 + tq, seq - DIL_NEAR), DIL_NEAR), DIL_NEAR,
                 lambda h: side_ref[h, right])
    o_ref[0] = jnp.concatenate([_normalize(acc) for _, acc in carry], axis=1).astype(o_ref.dtype)


def _attn_dil_call(q, k, v, st, bias, side):
    bsz, seq, qw = q.shape
    tile = lambda: pl.BlockSpec((1, ATT_TQ, qw), lambda b, i: (b, i, 0))
    full = lambda: pl.BlockSpec((1, seq, qw), lambda b, i: (b, 0, 0))
    return pl.pallas_call(
        _attn_dil_kernel,
        out_shape=jax.ShapeDtypeStruct((bsz, seq, GROUP_WIDTH), BF16),
        grid=(bsz, seq // ATT_TQ),
        in_specs=[tile(), full(), full(), tile(), _const_spec(bias.shape), _const_spec(side.shape)],
        out_specs=pl.BlockSpec((1, ATT_TQ, GROUP_WIDTH), lambda b, i: (b, i, 0)),
        compiler_params=_params(),
        name="attn_dilated",
    )(q, k, v, st, bias, side)


def _layer_norm(z, g, b):
    mu = jnp.mean(z, axis=-1, keepdims=True)
    zc = z - mu
    var = jnp.mean(zc * zc, axis=-1, keepdims=True)
    return zc * lax.rsqrt(var + LN_EPS) * g + b


def _outproj_kernel(ya_ref, yb_ref, yc_ref, yd_ref, x_ref, mod_ref, wo_ref, ln_ref, o_ref, *, alpha):
    half = x_ref.shape[1] // 2
    for r in (0, half):
        rows = slice(r, r + half)
        ycat = jnp.concatenate([ya_ref[0, rows], yb_ref[0, rows], yc_ref[0, rows], yd_ref[0, rows]], axis=1)
        y = jnp.dot(ycat, wo_ref[...], preferred_element_type=F32)
        z = alpha * x_ref[0, rows] + mod_ref[0, 2:3, :] * y
        o_ref[0, rows] = _layer_norm(z, ln_ref[0:1, :], ln_ref[1:2, :])


def _outproj_call(ys, x, mod, layer, wo, ln, *, alpha):
    bsz, seq, d = x.shape
    tm = ROW_TILE
    yspec = pl.BlockSpec((1, tm, GROUP_WIDTH), lambda b, i: (b, i, 0))
    xspec = pl.BlockSpec((1, tm, d), lambda b, i: (b, i, 0))
    return pl.pallas_call(
        functools.partial(_outproj_kernel, alpha=alpha),
        out_shape=jax.ShapeDtypeStruct((bsz, seq, d), F32),
        grid=(bsz, seq // tm),
        in_specs=[yspec, yspec, yspec, yspec, xspec,
                  _mod_spec(d, layer), _layer_spec(wo.shape, layer), _layer_spec(ln.shape, layer)],
        out_specs=xspec,
        compiler_params=_params(),
        name="out_proj_ln",
    )(*ys, x, mod, wo, ln)


def _mlp_kernel(x_ref, mod_ref, wup_ref, wdn_ref, ln_ref, o_ref, *, alpha):
    x = x_ref[0]
    h = (x * (1.0 + mod_ref[0, 4:5, :]) + mod_ref[0, 3:4, :]).astype(BF16)
    u = jnp.zeros(x.shape, F32)
    for c in range(MLP_HIDDEN // MLP_CHUNK):
        a = jnp.dot(h, wup_ref[:, c * MLP_CHUNK:(c + 1) * MLP_CHUNK], preferred_element_type=F32)
        a = jnp.maximum(a, 0.0)
        u = u + jnp.dot((a * a).astype(BF16), wdn_ref[c * MLP_CHUNK:(c + 1) * MLP_CHUNK, :],
                        preferred_element_type=F32)
    z = alpha * x + mod_ref[0, 5:6, :] * u
    o_ref[0] = _layer_norm(z, ln_ref[0:1, :], ln_ref[1:2, :])


def _mlp_call(x, mod, layer, wup, wdn, ln, *, alpha):
    bsz, seq, d = x.shape
    tm = ROW_TILE
    xspec = pl.BlockSpec((1, tm, d), lambda b, i: (b, i, 0))
    return pl.pallas_call(
        functools.partial(_mlp_kernel, alpha=alpha),
        out_shape=jax.ShapeDtypeStruct((bsz, seq, d), F32),
        grid=(bsz, seq // tm),
        in_specs=[xspec, _mod_spec(d, layer), _layer_spec(wup.shape, layer), _layer_spec(wdn.shape, layer),
                  _layer_spec(ln.shape, layer)],
        out_specs=xspec,
        compiler_params=_params(),
        name="mlp_ln",
    )(x, mod, wup, wdn, ln)


def _rot_half_source(width, block):
    half = block // 2
    d = np.arange(width)
    inner = d % block
    src = np.where(inner < half, d + half, d - half)
    sign = np.where(inner < half, -1.0, 1.0)
    return src, sign.astype(np.float32)


def _placement(n_in, n_out, pairs):
    m = np.zeros((n_in, n_out), np.float32)
    for r, c in pairs:
        m[r, c] = 1.0
    return jnp.asarray(m, BF16)


def _static_consts():
    pa = _placement(256, 1024, [(32 * hc + d, LANES * hc + d) for hc in range(8) for d in range(32)])
    pcq = _placement(256, 512, [(64 * h + d, LANES * h + d) for h in range(4) for d in range(64)])
    pck = _placement(128, 256, [(64 * g + d, LANES * g + d) for g in range(2) for d in range(64)])
    e = _placement(128, 512, [(s * 32 + d, LANES * h + MLA_NOPE_DIM + d)
                              for h in range(4) for s in range(2) for d in range(32)])
    g = np.zeros((256, 256), np.float32)
    for h in range(4):
        g[64 * h:64 * (h + 1), 64 * h:64 * (h + 1)] = 1.0 / HEAD_DIM
    sel = np.zeros((2, LANES, LANES), np.float32)
    sel[0, HEAD_DIM, :] = 1.0
    sel[1, 0:HEAD_DIM, 0:HEAD_DIM] = 1.0 / HEAD_DIM
    return dict(pa=pa, pcq=pcq, pck=pck, e=e, g=jnp.asarray(g, BF16), sel=jnp.asarray(sel, BF16))


def _rope_tables(pos, half):
    freqs = ROPE_THETA ** (-jnp.arange(half, dtype=F32) / half)
    ang = pos[:, None] * freqs[None, :]
    cos = jnp.concatenate([jnp.cos(ang), jnp.cos(ang)], axis=1)
    sin = jnp.concatenate([jnp.sin(ang), jnp.sin(ang)], axis=1)
    return cos, sin


def _position_tables(seq, slopes_a, slopes_d):
    pos = jnp.arange(seq, dtype=F32)
    row = jnp.floor(pos / GRID_W)
    col = pos - row * GRID_W
    cos32, sin32 = _rope_tables(pos, MLA_ROPE_DIM // 2)
    zeros32 = jnp.zeros((seq, 32), F32)
    ones64 = jnp.ones((seq, 64), F32)
    zeros64 = jnp.zeros((seq, 64), F32)
    b_scale = MLA_QK_DIM ** -0.5 * LOG2E
    cos_b = jnp.tile(jnp.concatenate([ones64, cos32, zeros32], 1), (1, 4)) * b_scale
    sin_b = jnp.tile(jnp.concatenate([zeros64, sin32, zeros32], 1), (1, 4)) * b_scale
    tkr = jnp.concatenate([cos32, sin32, zeros64], 1)
    cr, sr = _rope_tables(row, HEAD_DIM // 4)
    cc, sc = _rope_tables(col, HEAD_DIM // 4)
    cos_h = jnp.concatenate([cr, cc], 1)
    sin_h = jnp.concatenate([sr, sc], 1)
    c_scale = HEAD_DIM ** -0.5 * LOG2E
    tc = jnp.stack([jnp.tile(cos_h, (1, 4)), jnp.tile(sin_h, (1, 4))]) * c_scale
    tck = jnp.stack([jnp.tile(cos_h, (1, 2)), jnp.tile(sin_h, (1, 2))])
    ipos = np.arange(seq)
    hi = (ipos // POS_SPLIT).astype(np.float32)
    lo = (ipos % POS_SPLIT).astype(np.float32)
    qaug = np.zeros((seq, 8 * LANES), np.float32)
    kaug = np.zeros((seq, 8 * LANES), np.float32)
    terms, rest = [], LOG2E
    for _ in range(LOG2E_TERMS):
        t = float(np.float32(rest).astype(ml_dtypes.bfloat16))
        terms.append(t)
        rest -= t
    for hc in range(8):
        s = float(slopes_a[hc // 2])
        for n, t in enumerate(terms):
            base = LANES * hc + AUG_LANE + 4 * n
            qaug[:, base + 0] = hi
            qaug[:, base + 1] = lo
            qaug[:, base + 2] = s * t * POS_SPLIT
            qaug[:, base + 3] = s * t
            kaug[:, base + 0] = -s * t * POS_SPLIT
            kaug[:, base + 1] = -s * t
            kaug[:, base + 2] = hi
            kaug[:, base + 3] = lo
    a = np.arange(ATT_TQ)[:, None]

    def near_bias(o):
        cnt = np.zeros(o.shape, np.float32)
        for w, dil in DIL_PAIRS:
            cnt += ((np.abs(o) <= min(w // 2, DIL_NEAR)) & (o % dil == 0)).astype(np.float32)
        logc = np.log(np.maximum(cnt, 1.0))
        return np.stack([np.where(cnt > 0, (-float(s) * np.abs(o) + logc) * LOG2E, NEG_INF) for s in slopes_d])

    bias_d = near_bias(np.arange(ATT_TQ)[None, :] - a)
    side = np.arange(DIL_NEAR)[None, :]
    bias_side = np.stack([near_bias(side - DIL_NEAR - a), near_bias(side + ATT_TQ - a)], axis=1)
    bias_side = np.concatenate([bias_side, np.full_like(bias_side[:, :1], NEG_INF)], axis=1)
    w_far, stride = DIL_PAIRS[-1]
    assert stride == DIL_FAR_STRIDE and all(w // 2 <= DIL_NEAR for w, _ in DIL_PAIRS[:-1])
    members = np.arange(seq // stride)
    da = np.abs(members[None, :] - members[:, None])
    far = (da * stride > DIL_NEAR) & (da * stride <= w_far // 2)
    bias_far = np.stack([np.where(far, -float(s) * stride * da * LOG2E, NEG_INF) for s in slopes_d])
    b = np.arange(ATT_TK)[None, :]
    bias_a = np.stack([(-float(s) * LOG2E * np.abs(a - b)).astype(np.float32) for s in slopes_a])
    return dict(tb=jnp.stack([cos_b, sin_b]), tc=tc, tck=tck, tkr=tkr, bias_a=jnp.asarray(bias_a),
                qaug=jnp.asarray(qaug, BF16), kaug=jnp.asarray(kaug, BF16),
                bias_d=jnp.asarray(bias_d.astype(np.float32)), bias_side=jnp.asarray(bias_side.astype(np.float32)),
                bias_far=jnp.asarray(bias_far.astype(np.float32)))


class _ColumnPlan:
    def __init__(self):
        self.idx, self.sgn = [], []

    def plain(self, start, width):
        self.idx.extend(range(start, start + width))
        self.sgn.extend([1.0] * width)

    def zeros(self, width):
        self.idx.extend([0] * width)
        self.sgn.extend([0.0] * width)

    def take(self, indices):
        self.idx.extend(np.asarray(indices).tolist())
        self.sgn.extend([1.0] * len(indices))

    def partner(self, start, width, block):
        src, sign = _rot_half_source(width, block)
        self.idx.extend((start + src).tolist())
        self.sgn.extend(sign.tolist())

    def placed(self, start, n_heads, stride, width):
        for h in range(n_heads):
            self.plain(start + h * stride, width)
            self.zeros(LANES - width)

    def gather(self, w):
        idx, sgn = self.idx, self.sgn
        pieces, start = [], 0
        while start < len(idx):
            end = start + 1
            while end < len(idx) and sgn[end] == sgn[start] and (sgn[start] == 0 or idx[end] == idx[end - 1] + 1):
                end += 1
            if sgn[start] == 0:
                pieces.append(jnp.zeros(w.shape[:-1] + (end - start,), w.dtype))
            else:
                run = w[..., idx[start]:idx[start] + end - start]
                pieces.append(run if sgn[start] > 0 else -run)
            start = end
        return jnp.concatenate(pieces, axis=-1)


VEC_WIDTH = 512


def _stacked_weights(w_in, mla_w_uq, mla_w_ukv, mla_q_norm_g, mla_kv_norm_g, gqa_q_norm_g, gqa_k_norm_g):
    o_aq, o_ak, o_av, o_cq, o_ckv, o_kr = 0, 256, 512, 768, 1152, 1408
    o_gq, o_gk, o_gv, o_dq, o_dk, o_dv = 1440, 1696, 1824, 1952, 2208, 2464
    p = _ColumnPlan()
    p.plain(o_aq, 256); p.plain(o_ak, 256); p.plain(o_cq, 384); p.plain(o_ckv, 256)
    p.plain(o_kr, 32); p.partner(o_kr, 32, 32); p.zeros(64)
    p.plain(o_gq, 256); p.partner(o_gq, 256, 32); p.plain(o_gk, 128); p.partner(o_gk, 128, 32)
    p.plain(o_dq, 256); p.plain(o_dk, 256)
    p.placed(o_av, 4, 64, 64); p.placed(o_gv, 2, 64, 64); p.placed(o_dv, 4, 64, 64)
    assert len(p.idx) == W1_COLS
    w1 = p.gather(w_in).astype(BF16)
    p = _ColumnPlan()
    p.placed(0, 4, MLA_QK_DIM, MLA_QK_DIM)
    for h in range(4):
        p.zeros(MLA_NOPE_DIM)
        p.partner(h * MLA_QK_DIM + MLA_NOPE_DIM, MLA_ROPE_DIM, MLA_ROPE_DIM)
        p.zeros(LANES - MLA_QK_DIM)
    wuq = p.gather(mla_w_uq).astype(BF16)
    p = _ColumnPlan()
    p.placed(0, 4, MLA_NOPE_DIM + MLA_V_DIM, MLA_NOPE_DIM)
    p.placed(MLA_NOPE_DIM, 4, MLA_NOPE_DIM + MLA_V_DIM, MLA_V_DIM)
    wukv = p.gather(mla_w_ukv).astype(BF16)
    n_layers = w_in.shape[0]
    gains = jnp.concatenate([jnp.ones((n_layers, 1), F32), mla_q_norm_g.astype(F32), mla_kv_norm_g.astype(F32),
                             gqa_q_norm_g.astype(F32), gqa_k_norm_g.astype(F32)], axis=1)
    o_qn, o_kvn = 1, 1 + MLA_Q_RANK
    o_gqg, o_gkg = o_kvn + MLA_KV_RANK, o_kvn + MLA_KV_RANK + HEAD_DIM
    src64, _ = _rot_half_source(HEAD_DIM, HEAD_DIM // 2)
    p = _ColumnPlan()
    rows = (
        [("zeros", HEAD_DIM), ("plain", 0, 1), ("zeros", LANES - HEAD_DIM - 1)] * 4,
        [("plain", o_qn, MLA_Q_RANK)],
        [("plain", o_kvn, MLA_KV_RANK)],
        [("plain", o_gqg, HEAD_DIM)] * 4,
        [("take", o_gqg + src64)] * 4,
        [("plain", o_gkg, HEAD_DIM)] * 2,
        [("take", o_gkg + src64)] * 2,
        [("zeros", VEC_WIDTH)],
    )
    for entries in rows:
        for kind, *args in entries:
            getattr(p, kind)(*args)
        p.zeros(-len(p.idx) % VEC_WIDTH)
    vec = p.gather(gains).reshape(n_layers, len(rows), VEC_WIDTH)
    return dict(w1=w1, wuq=wuq, wukv=wukv, vec=vec)


def kernel(x, c, w_ada, b_ada, w_in, w_o, diff_lambda, diff_subln_g, mla_q_norm_g, mla_w_uq, mla_kv_norm_g, mla_w_ukv, gqa_q_norm_g, gqa_k_norm_g, ln_attn_g, ln_attn_b, w_up, w_down, ln_mlp_g, ln_mlp_b):
    alpha = (2 * DEPTH) ** 0.25
    seq = x.shape[1]
    n_slopes = 2 * N_HEADS
    slopes = [2.0 ** (-8.0 * (n + 1.0) / n_slopes) for n in range(n_slopes)]
    slopes_a, slopes_d = tuple(slopes[0::2]), tuple(slopes[1::2])
    consts = _static_consts()
    tables = _position_tables(seq, slopes_a, slopes_d)
    mod = _ada_call(c, w_ada, b_ada)
    layer_w = _stacked_weights(w_in, mla_w_uq, mla_w_ukv, mla_q_norm_g, mla_kv_norm_g,
                               gqa_q_norm_g, gqa_k_norm_g)
    lambda_init = np.asarray([0.8 - 0.6 * math.exp(-0.3 * l) for l in range(DEPTH)], np.float32)
    lam_pack = jnp.concatenate(
        [diff_lambda.astype(F32), jnp.broadcast_to(lambda_init[:, None, None], (DEPTH, 1, DIFF_QK_DIM))], axis=1)
    subln_g = jnp.pad(diff_subln_g.astype(F32), ((0, 0), (0, LANES - HEAD_DIM)))[:, None, :]
    wo, wup, wdn = w_o.astype(BF16), w_up.astype(BF16), w_down.astype(BF16)
    ln_attn = jnp.stack([ln_attn_g, ln_attn_b], axis=1)
    ln_mlp = jnp.stack([ln_mlp_g, ln_mlp_b], axis=1)
    for l in range(DEPTH):
        (qa, ka, va, qb, kb, vb, qc, kc, vc, qd, kd, vd) = _proj_call(x, mod, l, consts, layer_w, tables)
        ya = _attn_diff_call(qa, ka, va, l, lam_pack, subln_g, tables["bias_a"], consts["sel"])
        yb = _attn_plain_call(qb, kb, vb, n_heads=N_HEADS, q_per_kv=1)
        yc = _attn_plain_call(qc, kc, vc, n_heads=N_HEADS, q_per_kv=N_HEADS // GQA_KV_HEADS)
        yd = _attn_dil_call(qd, kd, vd, _attn_dil_far_call(qd, kd, vd, tables["bias_far"]),
                            tables["bias_d"], tables["bias_side"])
        x = _outproj_call((ya, yb, yc, yd), x, mod, l, wo, ln_attn, alpha=alpha)
        x = _mlp_call(x, mod, l, wup, wdn, ln_mlp, alpha=alpha)
    return x
```

```python
import functools
import math

import jax
import jax.numpy as jnp
import ml_dtypes
import numpy as np
from jax import lax
from jax.experimental import pallas as pl
from jax.experimental.pallas import tpu as pltpu

D_MODEL = 1024
DEPTH = 2
HEAD_DIM = 64
GROUP_WIDTH = D_MODEL // 4
N_HEADS = GROUP_WIDTH // HEAD_DIM
DIFF_QK_DIM = HEAD_DIM // 2
MLA_NOPE_DIM = HEAD_DIM
MLA_ROPE_DIM = HEAD_DIM // 2
MLA_V_DIM = HEAD_DIM
MLA_QK_DIM = MLA_NOPE_DIM + MLA_ROPE_DIM
MLA_Q_RANK = N_HEADS * MLA_QK_DIM
MLA_KV_RANK = 4 * MLA_V_DIM
GQA_KV_HEADS = N_HEADS // 2
GRID_W = 64
DIL_PAIRS = ((128, 1), (512, 4), (2048, 16))
MLP_HIDDEN = 4 * D_MODEL
ROPE_THETA = 10000.0
RMS_EPS = 1e-6
LN_EPS = 1e-5
NEG_INF = -1e30
LOG2E = math.log2(math.e)

LANES = 128
VMEM_LIMIT_BYTES = 56 * 1024 * 1024

PROJ_ROWS = 512
ATT_TQ = 512
ATT_TK = 512
ROW_TILE = 512
MLP_CHUNK = 1024
AUG_LANE = DIFF_QK_DIM
LOG2E_TERMS = 5
POS_SPLIT = 64
DIFF_PAIR = 2
DIL_BAND = 1024

_NT = (((1,), (1,)), ((), ()))

F32 = jnp.float32
BF16 = jnp.bfloat16


def _params(**kw):
    return pltpu.CompilerParams(vmem_limit_bytes=VMEM_LIMIT_BYTES, **kw)


def _const_spec(shape):
    nd = len(shape)
    return pl.BlockSpec(shape, lambda *_: (0,) * nd, pipeline_mode=pl.Buffered(1))


def _layer_spec(shape, layer):
    nd = len(shape)
    return pl.BlockSpec((None,) + tuple(shape[1:]), lambda *_: (layer,) + (0,) * (nd - 1),
                        pipeline_mode=pl.Buffered(1))


def _mod_spec(d, layer):
    return pl.BlockSpec((None, 1, 6, d), lambda b, *_: (layer, b, 0, 0))


def _ada_kernel(c_ref, w_ref, b_ref, o_ref):
    c = c_ref[...]
    cond = c * (1.0 / (1.0 + jnp.exp(-c)))
    o_ref[0, 0] = jnp.dot(cond, w_ref[0], preferred_element_type=F32) + b_ref[0, 0]


def _ada_call(c, w_ada, b_ada):
    n_layers, d, _ = w_ada.shape
    bsz = c.shape[0]
    b4 = b_ada.reshape(n_layers, 6, 1, d)
    out = pl.pallas_call(
        _ada_kernel,
        out_shape=jax.ShapeDtypeStruct((n_layers, 6, bsz, d), F32),
        grid=(n_layers, 6),
        in_specs=[
            pl.BlockSpec((bsz, d), lambda l, j: (0, 0)),
            pl.BlockSpec((1, d, d), lambda l, j: (l, 0, j)),
            pl.BlockSpec((1, 1, 1, d), lambda l, j: (l, j, 0, 0)),
        ],
        out_specs=pl.BlockSpec((1, 1, bsz, d), lambda l, j: (l, j, 0, 0)),
        compiler_params=_params(),
        name="adaln_mod",
    )(c, w_ada, b4)
    return jnp.transpose(out, (0, 2, 1, 3))


_SEG_WIDTHS = (
    ("a_q", 256), ("a_k", 256), ("b_cq", 384), ("b_ckv", 256), ("b_kr", 128),
    ("c_q", 256), ("c_q_sw", 256), ("c_k", 128), ("c_k_sw", 128),
    ("d_q", 256), ("d_k", 256), ("v_a", 512), ("v_c", 256), ("v_d", 512),
)
_SEG = {}
_off = 0
for _name, _w in _SEG_WIDTHS:
    _SEG[_name] = (_off, _off + _w)
    _off += _w
W1_COLS = _off


def _lane_spread(x, mat):
    hi = x.astype(BF16)
    lo = (x - hi.astype(F32)).astype(BF16)
    return (jnp.dot(hi, mat, preferred_element_type=F32)
            + jnp.dot(lo, mat, preferred_element_type=F32))


def _rms_rows(x, gsum):
    return lax.rsqrt(_lane_spread(x * x, gsum) + RMS_EPS)


def _proj_kernel(x_ref, mod_ref, w1_ref, pa_ref, wuq_ref, wukv_ref, e_ref, g_ref, pcq_ref, pck_ref,
                 vec_ref, tb_ref, tc_ref, tck_ref, tkr_ref, qaug_ref, kaug_ref,
                 qa_ref, ka_ref, va_ref, qb_ref, kb_ref, vb_ref, qc_ref, kc_ref, vc_ref,
                 qd_ref, kd_ref, vd_ref):
    sh = mod_ref[0, 0:1, :]
    sc = mod_ref[0, 1:2, :]
    h = (x_ref[0] * (1.0 + sc) + sh).astype(BF16)

    def seg(name):
        a, b = _SEG[name]
        return jnp.dot(h, w1_ref[:, a:b], preferred_element_type=F32)

    def vec(row, width):
        return vec_ref[row:row + 1, 0:width]

    a_scale = DIFF_QK_DIM ** -0.5 * LOG2E
    qa = jnp.dot((seg("a_q") * a_scale).astype(BF16), pa_ref[...], preferred_element_type=F32)
    qa_ref[0] = qa.astype(BF16) + qaug_ref[...]
    ka = jnp.dot(seg("a_k").astype(BF16), pa_ref[...], preferred_element_type=F32)
    ka_ref[0] = ka.astype(BF16) + kaug_ref[...]
    va_ref[0] = (seg("v_a") + vec(0, 512)).astype(BF16)

    cq = seg("b_cq")
    r = lax.rsqrt(jnp.mean(cq * cq, axis=-1, keepdims=True) + RMS_EPS)
    cqn = (cq * r * vec(1, 384)).astype(BF16)
    q2 = jnp.dot(cqn, wuq_ref[...], preferred_element_type=F32)
    qb_ref[0] = (q2[:, 0:512] * tb_ref[0] + q2[:, 512:1024] * tb_ref[1]).astype(BF16)
    ckv = seg("b_ckv")
    r = lax.rsqrt(jnp.mean(ckv * ckv, axis=-1, keepdims=True) + RMS_EPS)
    ckvn = (ckv * r * vec(2, 256)).astype(BF16)
    kv2 = jnp.dot(ckvn, wukv_ref[...], preferred_element_type=F32)
    kr = (seg("b_kr") * tkr_ref[...]).astype(BF16)
    kb_ref[0] = (kv2[:, 0:512] + jnp.dot(kr, e_ref[...], preferred_element_type=F32)).astype(BF16)
    vb_ref[0] = (kv2[:, 512:1024] + vec(0, 512)).astype(BF16)

    cqx = seg("c_q")
    r = _rms_rows(cqx, g_ref[...])
    qc = cqx * r * vec(3, 256) * tc_ref[0] + seg("c_q_sw") * r * vec(4, 256) * tc_ref[1]
    qc_ref[0] = jnp.dot(qc.astype(BF16), pcq_ref[...], preferred_element_type=F32).astype(BF16)
    ckx = seg("c_k")
    r = _rms_rows(ckx, g_ref[0:128, 0:128])
    kc = ckx * r * vec(5, 128) * tck_ref[0] + seg("c_k_sw") * r * vec(6, 128) * tck_ref[1]
    kc_ref[0] = jnp.dot(kc.astype(BF16), pck_ref[...], preferred_element_type=F32).astype(BF16)
    vc_ref[0] = (seg("v_c") + vec(0, 256)).astype(BF16)

    d_scale = HEAD_DIM ** -0.5 * LOG2E
    qd = jnp.dot((seg("d_q") * d_scale).astype(BF16), pcq_ref[...], preferred_element_type=F32)
    qd_ref[0] = qd.astype(BF16)
    kd_ref[0] = jnp.dot(seg("d_k").astype(BF16), pcq_ref[...], preferred_element_type=F32).astype(BF16)
    vd_ref[0] = (seg("v_d") + vec(0, 512)).astype(BF16)


def _proj_call(x, mod, layer, consts, layer_w, tables):
    w1 = layer_w["w1"]
    bsz, seq, d = x.shape
    tm = PROJ_ROWS
    nt = seq // tm
    out_widths = (1024, 1024, 512, 512, 512, 512, 512, 256, 256, 512, 512, 512)
    row_spec = lambda w: pl.BlockSpec((1, tm, w), lambda b, i: (b, i, 0))
    tab3 = lambda w: pl.BlockSpec((2, tm, w), lambda b, i: (0, i, 0))
    tab2 = lambda w: pl.BlockSpec((tm, w), lambda b, i: (i, 0))
    in_specs = [
        row_spec(d),
        _mod_spec(d, layer),
        _layer_spec(w1.shape, layer),
        _const_spec(consts["pa"].shape),
        _layer_spec(layer_w["wuq"].shape, layer),
        _layer_spec(layer_w["wukv"].shape, layer),
        _const_spec(consts["e"].shape),
        _const_spec(consts["g"].shape),
        _const_spec(consts["pcq"].shape),
        _const_spec(consts["pck"].shape),
        _layer_spec(layer_w["vec"].shape, layer),
        tab3(512), tab3(256), tab3(128), tab2(128), tab2(1024), tab2(1024),
    ]
    return pl.pallas_call(
        _proj_kernel,
        out_shape=tuple(jax.ShapeDtypeStruct((bsz, seq, w), BF16) for w in out_widths),
        grid=(bsz, nt),
        in_specs=in_specs,
        out_specs=tuple(row_spec(w) for w in out_widths),
        compiler_params=_params(),
        name="proj_prep",
    )(x, mod, w1, consts["pa"], layer_w["wuq"], layer_w["wukv"], consts["e"], consts["g"],
      consts["pcq"], consts["pck"], layer_w["vec"], tables["tb"], tables["tc"], tables["tck"],
      tables["tkr"], tables["qaug"], tables["kaug"])


def _softmax_steps(scores, vs, carry):
    stats = []
    for s, (m, _) in zip(scores, carry):
        m_new = jnp.maximum(m, jnp.max(s, axis=-1, keepdims=True))
        stats.append((m_new, jnp.exp2(s - m_new).astype(BF16), jnp.exp2(m - m_new)))
    out = []
    for (m_new, p, alpha), v, (_, acc) in zip(stats, vs, carry):
        out.append((m_new, alpha * acc + jnp.dot(p, v, preferred_element_type=F32)))
    return tuple(out)


def _softmax_init(tq):
    return jnp.full((tq, 1), NEG_INF, F32), jnp.zeros((tq, LANES), F32)


def _normalize(acc):
    return acc[:, 0:HEAD_DIM] / acc[:, HEAD_DIM:HEAD_DIM + 1]


def _chunk(ref, off, group):
    return ref[0, pl.ds(off, ATT_TK), LANES * group:LANES * (group + 1)]


def _attn_plain_kernel(q_ref, k_ref, v_ref, o_ref, *, n_heads, q_per_kv):
    tq = q_ref.shape[1]
    nk = k_ref.shape[1] // ATT_TK
    carry = tuple(_softmax_init(tq) for _ in range(n_heads))
    for j in range(nk):
        off = j * ATT_TK
        scores = [lax.dot_general(q_ref[0, :, LANES * h:LANES * (h + 1)], _chunk(k_ref, off, h // q_per_kv),
                                  _NT, preferred_element_type=F32) for h in range(n_heads)]
        vs = [_chunk(v_ref, off, h // q_per_kv) for h in range(n_heads)]
        carry = _softmax_steps(scores, vs, carry)
    o_ref[0] = jnp.concatenate([_normalize(acc) for _, acc in carry], axis=1).astype(o_ref.dtype)


def _attn_plain_call(q, k, v, *, n_heads, q_per_kv):
    bsz, seq, qw = q.shape
    kw = k.shape[2]
    kern = functools.partial(_attn_plain_kernel, n_heads=n_heads, q_per_kv=q_per_kv)
    return pl.pallas_call(
        kern,
        out_shape=jax.ShapeDtypeStruct((bsz, seq, n_heads * HEAD_DIM), BF16),
        grid=(bsz, seq // ATT_TQ),
        in_specs=[
            pl.BlockSpec((1, ATT_TQ, qw), lambda b, i: (b, i, 0)),
            pl.BlockSpec((1, seq, kw), lambda b, i: (b, 0, 0)),
            pl.BlockSpec((1, seq, kw), lambda b, i: (b, 0, 0)),
        ],
        out_specs=pl.BlockSpec((1, ATT_TQ, n_heads * HEAD_DIM), lambda b, i: (b, i, 0)),
        compiler_params=_params(),
        name="attn_plain",
    )(q, k, v)


def _attn_diff_kernel(q_ref, k_ref, v_ref, lam_ref, g_ref, bias_ref, sel_ref, o_ref):
    tq = q_ref.shape[1]
    nk = k_ref.shape[1] // ATT_TK
    qi = pl.program_id(2)
    n_comp = 2 * DIFF_PAIR
    lane = lax.broadcasted_iota(jnp.int32, (1, LANES), 1)

    def step(off, carry, aug_sign, diagonal):
        aug = jnp.where(lane < AUG_LANE, 1.0, aug_sign).astype(BF16)
        scores = []
        for hc in range(n_comp):
            q = q_ref[0, :, LANES * hc:LANES * (hc + 1)] * aug
            s = lax.dot_general(q, _chunk(k_ref, off, hc), _NT, preferred_element_type=F32)
            if diagonal:
                s = s + bias_ref[hc // 2]
            scores.append(s)
        vs = [_chunk(v_ref, off, hc // 2) for hc in range(n_comp)]
        return _softmax_steps(scores, vs, carry)

    carry = step(pl.multiple_of(qi * ATT_TK, ATT_TK), tuple(_softmax_init(tq) for _ in range(n_comp)),
                 0.0, True)
    for j in range(nk - 1):
        jj = j + (j >= qi).astype(jnp.int32)
        sign = jnp.where(jj < qi, 1.0, -1.0)
        carry = step(pl.multiple_of(jj * ATT_TK, ATT_TK), carry, sign, False)

    lf = lam_ref[...]
    lambda_init = lf[4:5, 0:1]
    lam = (jnp.exp(jnp.sum(lf[0:1] * lf[1:2], axis=-1, keepdims=True))
           - jnp.exp(jnp.sum(lf[2:3] * lf[3:4], axis=-1, keepdims=True)) + lambda_init)
    outs = []
    for h in range(DIFF_PAIR):
        a0, a1 = carry[2 * h][1], carry[2 * h + 1][1]
        y = a0 / _lane_spread(a0, sel_ref[0]) - lam * (a1 / _lane_spread(a1, sel_ref[0]))
        r = lax.rsqrt(_lane_spread(y * y, sel_ref[1]) + RMS_EPS)
        outs.append((y * r * g_ref[...] * (1.0 - lambda_init))[:, 0:HEAD_DIM])
    o_ref[0] = jnp.concatenate(outs, axis=1).astype(o_ref.dtype)


def _attn_diff_call(q, k, v, layer, lam, subln_g, bias, sel):
    bsz, seq, _ = q.shape
    qw = 2 * DIFF_PAIR * LANES
    vw = DIFF_PAIR * LANES
    ow = DIFF_PAIR * HEAD_DIM
    return pl.pallas_call(
        _attn_diff_kernel,
        out_shape=jax.ShapeDtypeStruct((bsz, seq, GROUP_WIDTH), BF16),
        grid=(bsz, N_HEADS // DIFF_PAIR, seq // ATT_TQ),
        in_specs=[
            pl.BlockSpec((1, ATT_TQ, qw), lambda b, p, i: (b, i, p)),
            pl.BlockSpec((1, seq, qw), lambda b, p, i: (b, 0, p)),
            pl.BlockSpec((1, seq, vw), lambda b, p, i: (b, 0, p)),
            _layer_spec(lam.shape, layer),
            _layer_spec(subln_g.shape, layer),
            pl.BlockSpec((DIFF_PAIR, ATT_TQ, ATT_TK), lambda b, p, i: (p, 0, 0)),
            _const_spec(sel.shape),
        ],
        out_specs=pl.BlockSpec((1, ATT_TQ, ow), lambda b, p, i: (b, i, p)),
        compiler_params=_params(),
        name="attn_diff",
    )(q, k, v, lam, subln_g, bias, sel)


def _attn_dil_kernel(q_ref, k_ref, v_ref, bias_ref, o_ref):
    tq = q_ref.shape[1]
    nk = k_ref.shape[1] // ATT_TK
    reach = DIL_BAND // ATT_TK
    qi = pl.program_id(1)
    n_off = 2 * reach + 1
    carry = tuple(_softmax_init(tq) for _ in range(N_HEADS))
    for t in [reach] + [t for t in range(n_off) if t != reach]:
        j = qi + (t - reach)
        valid = jnp.logical_and(j >= 0, j < nk)
        off = pl.multiple_of(jnp.clip(j, 0, nk - 1) * ATT_TK, ATT_TK)
        tb = jnp.where(valid, t, n_off)
        scores = [lax.dot_general(q_ref[0, :, LANES * h:LANES * (h + 1)], _chunk(k_ref, off, h), _NT,
                                  preferred_element_type=F32) + bias_ref[h, tb] for h in range(N_HEADS)]
        vs = [_chunk(v_ref, off, h) for h in range(N_HEADS)]
        carry = _softmax_steps(scores, vs, carry)
    o_ref[0] = jnp.concatenate([_normalize(acc) for _, acc in carry], axis=1).astype(o_ref.dtype)


def _attn_dil_call(q, k, v, bias):
    bsz, seq, qw = q.shape
    return pl.pallas_call(
        _attn_dil_kernel,
        out_shape=jax.ShapeDtypeStruct((bsz, seq, GROUP_WIDTH), BF16),
        grid=(bsz, seq // ATT_TQ),
        in_specs=[
            pl.BlockSpec((1, ATT_TQ, qw), lambda b, i: (b, i, 0)),
            pl.BlockSpec((1, seq, qw), lambda b, i: (b, 0, 0)),
            pl.BlockSpec((1, seq, qw), lambda b, i: (b, 0, 0)),
            _const_spec(bias.shape),
        ],
        out_specs=pl.BlockSpec((1, ATT_TQ, GROUP_WIDTH), lambda b, i: (b, i, 0)),
        compiler_params=_params(),
        name="attn_dilated",
    )(q, k, v, bias)


def _layer_norm(z, g, b):
    mu = jnp.mean(z, axis=-1, keepdims=True)
    zc = z - mu
    var = jnp.mean(zc * zc, axis=-1, keepdims=True)
    return zc * lax.rsqrt(var + LN_EPS) * g + b


def _outproj_kernel(ya_ref, yb_ref, yc_ref, yd_ref, x_ref, mod_ref, wo_ref, ln_ref, o_ref, *, alpha):
    half = x_ref.shape[1] // 2
    for r in (0, half):
        rows = slice(r, r + half)
        ycat = jnp.concatenate([ya_ref[0, rows], yb_ref[0, rows], yc_ref[0, rows], yd_ref[0, rows]], axis=1)
        y = jnp.dot(ycat, wo_ref[...], preferred_element_type=F32)
        z = alpha * x_ref[0, rows] + mod_ref[0, 2:3, :] * y
        o_ref[0, rows] = _layer_norm(z, ln_ref[0:1, :], ln_ref[1:2, :])


def _outproj_call(ys, x, mod, layer, wo, ln, *, alpha):
    bsz, seq, d = x.shape
    tm = ROW_TILE
    yspec = pl.BlockSpec((1, tm, GROUP_WIDTH), lambda b, i: (b, i, 0))
    xspec = pl.BlockSpec((1, tm, d), lambda b, i: (b, i, 0))
    return pl.pallas_call(
        functools.partial(_outproj_kernel, alpha=alpha),
        out_shape=jax.ShapeDtypeStruct((bsz, seq, d), F32),
        grid=(bsz, seq // tm),
        in_specs=[yspec, yspec, yspec, yspec, xspec,
                  _mod_spec(d, layer), _layer_spec(wo.shape, layer), _layer_spec(ln.shape, layer)],
        out_specs=xspec,
        compiler_params=_params(),
        name="out_proj_ln",
    )(*ys, x, mod, wo, ln)


def _mlp_kernel(x_ref, mod_ref, wup_ref, wdn_ref, ln_ref, o_ref, *, alpha):
    x = x_ref[0]
    h = (x * (1.0 + mod_ref[0, 4:5, :]) + mod_ref[0, 3:4, :]).astype(BF16)
    u = jnp.zeros(x.shape, F32)
    for c in range(MLP_HIDDEN // MLP_CHUNK):
        a = jnp.dot(h, wup_ref[:, c * MLP_CHUNK:(c + 1) * MLP_CHUNK], preferred_element_type=F32)
        a = jnp.maximum(a, 0.0)
        u = u + jnp.dot((a * a).astype(BF16), wdn_ref[c * MLP_CHUNK:(c + 1) * MLP_CHUNK, :],
                        preferred_element_type=F32)
    z = alpha * x + mod_ref[0, 5:6, :] * u
    o_ref[0] = _layer_norm(z, ln_ref[0:1, :], ln_ref[1:2, :])


def _mlp_call(x, mod, layer, wup, wdn, ln, *, alpha):
    bsz, seq, d = x.shape
    tm = ROW_TILE
    xspec = pl.BlockSpec((1, tm, d), lambda b, i: (b, i, 0))
    return pl.pallas_call(
        functools.partial(_mlp_kernel, alpha=alpha),
        out_shape=jax.ShapeDtypeStruct((bsz, seq, d), F32),
        grid=(bsz, seq // tm),
        in_specs=[xspec, _mod_spec(d, layer), _layer_spec(wup.shape, layer), _layer_spec(wdn.shape, layer),
                  _layer_spec(ln.shape, layer)],
        out_specs=xspec,
        compiler_params=_params(),
        name="mlp_ln",
    )(x, mod, wup, wdn, ln)


def _rot_half_source(width, block):
    half = block // 2
    d = np.arange(width)
    inner = d % block
    src = np.where(inner < half, d + half, d - half)
    sign = np.where(inner < half, -1.0, 1.0)
    return src, sign.astype(np.float32)


def _placement(n_in, n_out, pairs):
    m = np.zeros((n_in, n_out), np.float32)
    for r, c in pairs:
        m[r, c] = 1.0
    return jnp.asarray(m, BF16)


def _static_consts():
    pa = _placement(256, 1024, [(32 * hc + d, LANES * hc + d) for hc in range(8) for d in range(32)])
    pcq = _placement(256, 512, [(64 * h + d, LANES * h + d) for h in range(4) for d in range(64)])
    pck = _placement(128, 256, [(64 * g + d, LANES * g + d) for g in range(2) for d in range(64)])
    e = _placement(128, 512, [(s * 32 + d, LANES * h + MLA_NOPE_DIM + d)
                              for h in range(4) for s in range(2) for d in range(32)])
    g = np.zeros((256, 256), np.float32)
    for h in range(4):
        g[64 * h:64 * (h + 1), 64 * h:64 * (h + 1)] = 1.0 / HEAD_DIM
    sel = np.zeros((2, LANES, LANES), np.float32)
    sel[0, HEAD_DIM, :] = 1.0
    sel[1, 0:HEAD_DIM, 0:HEAD_DIM] = 1.0 / HEAD_DIM
    return dict(pa=pa, pcq=pcq, pck=pck, e=e, g=jnp.asarray(g, BF16), sel=jnp.asarray(sel, BF16))


def _rope_tables(pos, half):
    freqs = ROPE_THETA ** (-jnp.arange(half, dtype=F32) / half)
    ang = pos[:, None] * freqs[None, :]
    cos = jnp.concatenate([jnp.cos(ang), jnp.cos(ang)], axis=1)
    sin = jnp.concatenate([jnp.sin(ang), jnp.sin(ang)], axis=1)
    return cos, sin


def _position_tables(seq, slopes_a, slopes_d):
    pos = jnp.arange(seq, dtype=F32)
    row = jnp.floor(pos / GRID_W)
    col = pos - row * GRID_W
    cos32, sin32 = _rope_tables(pos, MLA_ROPE_DIM // 2)
    zeros32 = jnp.zeros((seq, 32), F32)
    ones64 = jnp.ones((seq, 64), F32)
    zeros64 = jnp.zeros((seq, 64), F32)
    b_scale = MLA_QK_DIM ** -0.5 * LOG2E
    cos_b = jnp.tile(jnp.concatenate([ones64, cos32, zeros32], 1), (1, 4)) * b_scale
    sin_b = jnp.tile(jnp.concatenate([zeros64, sin32, zeros32], 1), (1, 4)) * b_scale
    tkr = jnp.concatenate([cos32, sin32, zeros64], 1)
    cr, sr = _rope_tables(row, HEAD_DIM // 4)
    cc, sc = _rope_tables(col, HEAD_DIM // 4)
    cos_h = jnp.concatenate([cr, cc], 1)
    sin_h = jnp.concatenate([sr, sc], 1)
    c_scale = HEAD_DIM ** -0.5 * LOG2E
    tc = jnp.stack([jnp.tile(cos_h, (1, 4)), jnp.tile(sin_h, (1, 4))]) * c_scale
    tck = jnp.stack([jnp.tile(cos_h, (1, 2)), jnp.tile(sin_h, (1, 2))])
    ipos = np.arange(seq)
    hi = (ipos // POS_SPLIT).astype(np.float32)
    lo = (ipos % POS_SPLIT).astype(np.float32)
    qaug = np.zeros((seq, 8 * LANES), np.float32)
    kaug = np.zeros((seq, 8 * LANES), np.float32)
    terms, rest = [], LOG2E
    for _ in range(LOG2E_TERMS):
        t = float(np.float32(rest).astype(ml_dtypes.bfloat16))
        terms.append(t)
        rest -= t
    for hc in range(8):
        s = float(slopes_a[hc // 2])
        for n, t in enumerate(terms):
            base = LANES * hc + AUG_LANE + 4 * n
            qaug[:, base + 0] = hi
            qaug[:, base + 1] = lo
            qaug[:, base + 2] = s * t * POS_SPLIT
            qaug[:, base + 3] = s * t
            kaug[:, base + 0] = -s * t * POS_SPLIT
            kaug[:, base + 1] = -s * t
            kaug[:, base + 2] = hi
            kaug[:, base + 3] = lo
    reach = DIL_BAND // ATT_TK
    a = np.arange(ATT_TQ)[:, None]
    b = np.arange(ATT_TK)[None, :]
    tabs = []
    for t in range(2 * reach + 1):
        o = (t - reach) * ATT_TK + b - a
        cnt = np.zeros(o.shape, np.float32)
        for w, dil in DIL_PAIRS:
            cnt += ((np.abs(o) <= w // 2) & (o % dil == 0)).astype(np.float32)
        tabs.append((np.abs(o).astype(np.float32), cnt))
    dist = np.stack([t[0] for t in tabs])
    cnt = np.stack([t[1] for t in tabs])
    logc = np.where(cnt > 0, np.log(np.maximum(cnt, 1.0)), NEG_INF).astype(np.float32)
    bias_d = np.stack([np.where(cnt > 0, (-float(s) * dist + logc) * LOG2E, NEG_INF) for s in slopes_d])
    bias_d = np.concatenate([bias_d, np.full_like(bias_d[:, :1], NEG_INF)], axis=1)
    bias_a = np.stack([(-float(s) * LOG2E * np.abs(a - b)).astype(np.float32) for s in slopes_a])
    return dict(tb=jnp.stack([cos_b, sin_b]), tc=tc, tck=tck, tkr=tkr, bias_a=jnp.asarray(bias_a),
                qaug=jnp.asarray(qaug, BF16), kaug=jnp.asarray(kaug, BF16),
                bias_d=jnp.asarray(bias_d.astype(np.float32)))


class _ColumnPlan:
    def __init__(self):
        self.idx, self.sgn = [], []

    def plain(self, start, width):
        self.idx.extend(range(start, start + width))
        self.sgn.extend([1.0] * width)

    def zeros(self, width):
        self.idx.extend([0] * width)
        self.sgn.extend([0.0] * width)

    def take(self, indices):
        self.idx.extend(np.asarray(indices).tolist())
        self.sgn.extend([1.0] * len(indices))

    def partner(self, start, width, block):
        src, sign = _rot_half_source(width, block)
        self.idx.extend((start + src).tolist())
        self.sgn.extend(sign.tolist())

    def placed(self, start, n_heads, stride, width):
        for h in range(n_heads):
            self.plain(start + h * stride, width)
            self.zeros(LANES - width)

    def gather(self, w):
        idx, sgn = self.idx, self.sgn
        pieces, start = [], 0
        while start < len(idx):
            end = start + 1
            while end < len(idx) and sgn[end] == sgn[start] and (sgn[start] == 0 or idx[end] == idx[end - 1] + 1):
                end += 1
            if sgn[start] == 0:
                pieces.append(jnp.zeros(w.shape[:-1] + (end - start,), w.dtype))
            else:
                run = w[..., idx[start]:idx[start] + end - start]
                pieces.append(run if sgn[start] > 0 else -run)
            start = end
        return jnp.concatenate(pieces, axis=-1)


VEC_WIDTH = 512


def _stacked_weights(w_in, mla_w_uq, mla_w_ukv, mla_q_norm_g, mla_kv_norm_g, gqa_q_norm_g, gqa_k_norm_g):
    o_aq, o_ak, o_av, o_cq, o_ckv, o_kr = 0, 256, 512, 768, 1152, 1408
    o_gq, o_gk, o_gv, o_dq, o_dk, o_dv = 1440, 1696, 1824, 1952, 2208, 2464
    p = _ColumnPlan()
    p.plain(o_aq, 256); p.plain(o_ak, 256); p.plain(o_cq, 384); p.plain(o_ckv, 256)
    p.plain(o_kr, 32); p.partner(o_kr, 32, 32); p.zeros(64)
    p.plain(o_gq, 256); p.partner(o_gq, 256, 32); p.plain(o_gk, 128); p.partner(o_gk, 128, 32)
    p.plain(o_dq, 256); p.plain(o_dk, 256)
    p.placed(o_av, 4, 64, 64); p.placed(o_gv, 2, 64, 64); p.placed(o_dv, 4, 64, 64)
    assert len(p.idx) == W1_COLS
    w1 = p.gather(w_in).astype(BF16)
    p = _ColumnPlan()
    p.placed(0, 4, MLA_QK_DIM, MLA_QK_DIM)
    for h in range(4):
        p.zeros(MLA_NOPE_DIM)
        p.partner(h * MLA_QK_DIM + MLA_NOPE_DIM, MLA_ROPE_DIM, MLA_ROPE_DIM)
        p.zeros(LANES - MLA_QK_DIM)
    wuq = p.gather(mla_w_uq).astype(BF16)
    p = _ColumnPlan()
    p.placed(0, 4, MLA_NOPE_DIM + MLA_V_DIM, MLA_NOPE_DIM)
    p.placed(MLA_NOPE_DIM, 4, MLA_NOPE_DIM + MLA_V_DIM, MLA_V_DIM)
    wukv = p.gather(mla_w_ukv).astype(BF16)
    n_layers = w_in.shape[0]
    gains = jnp.concatenate([jnp.ones((n_layers, 1), F32), mla_q_norm_g.astype(F32), mla_kv_norm_g.astype(F32),
                             gqa_q_norm_g.astype(F32), gqa_k_norm_g.astype(F32)], axis=1)
    o_qn, o_kvn = 1, 1 + MLA_Q_RANK
    o_gqg, o_gkg = o_kvn + MLA_KV_RANK, o_kvn + MLA_KV_RANK + HEAD_DIM
    src64, _ = _rot_half_source(HEAD_DIM, HEAD_DIM // 2)
    p = _ColumnPlan()
    rows = (
        [("zeros", HEAD_DIM), ("plain", 0, 1), ("zeros", LANES - HEAD_DIM - 1)] * 4,
        [("plain", o_qn, MLA_Q_RANK)],
        [("plain", o_kvn, MLA_KV_RANK)],
        [("plain", o_gqg, HEAD_DIM)] * 4,
        [("take", o_gqg + src64)] * 4,
        [("plain", o_gkg, HEAD_DIM)] * 2,
        [("take", o_gkg + src64)] * 2,
        [("zeros", VEC_WIDTH)],
    )
    for entries in rows:
        for kind, *args in entries:
            getattr(p, kind)(*args)
        p.zeros(-len(p.idx) % VEC_WIDTH)
    vec = p.gather(gains).reshape(n_layers, len(rows), VEC_WIDTH)
    return dict(w1=w1, wuq=wuq, wukv=wukv, vec=vec)


def kernel(x, c, w_ada, b_ada, w_in, w_o, diff_lambda, diff_subln_g, mla_q_norm_g, mla_w_uq, mla_kv_norm_g, mla_w_ukv, gqa_q_norm_g, gqa_k_norm_g, ln_attn_g, ln_attn_b, w_up, w_down, ln_mlp_g, ln_mlp_b):
    alpha = (2 * DEPTH) ** 0.25
    seq = x.shape[1]
    n_slopes = 2 * N_HEADS
    slopes = [2.0 ** (-8.0 * (n + 1.0) / n_slopes) for n in range(n_slopes)]
    slopes_a, slopes_d = tuple(slopes[0::2]), tuple(slopes[1::2])
    consts = _static_consts()
    tables = _position_tables(seq, slopes_a, slopes_d)
    mod = _ada_call(c, w_ada, b_ada)
    layer_w = _stacked_weights(w_in, mla_w_uq, mla_w_ukv, mla_q_norm_g, mla_kv_norm_g,
                               gqa_q_norm_g, gqa_k_norm_g)
    lambda_init = np.asarray([0.8 - 0.6 * math.exp(-0.3 * l) for l in range(DEPTH)], np.float32)
    lam_pack = jnp.concatenate(
        [diff_lambda.astype(F32), jnp.broadcast_to(lambda_init[:, None, None], (DEPTH, 1, DIFF_QK_DIM))], axis=1)
    subln_g = jnp.pad(diff_subln_g.astype(F32), ((0, 0), (0, LANES - HEAD_DIM)))[:, None, :]
    wo, wup, wdn = w_o.astype(BF16), w_up.astype(BF16), w_down.astype(BF16)
    ln_attn = jnp.stack([ln_attn_g, ln_attn_b], axis=1)
    ln_mlp = jnp.stack([ln_mlp_g, ln_mlp_b], axis=1)
    for l in range(DEPTH):
        (qa, ka, va, qb, kb, vb, qc, kc, vc, qd, kd, vd) = _proj_call(x, mod, l, consts, layer_w, tables)
        ya = _attn_diff_call(qa, ka, va, l, lam_pack, subln_g, tables["bias_a"], consts["sel"])
        yb = _attn_plain_call(qb, kb, vb, n_heads=N_HEADS, q_per_kv=1)
        yc = _attn_plain_call(qc, kc, vc, n_heads=N_HEADS, q_per_kv=N_HEADS // GQA_KV_HEADS)
        yd = _attn_dil_call(qd, kd, vd, tables["bias_d"])
        x = _outproj_call((ya, yb, yc, yd), x, mod, l, wo, ln_attn, alpha=alpha)
        x = _mlp_call(x, mod, l, wup, wdn, ln_mlp, alpha=alpha)
    return x
```

```python
import functools
import math

import jax
import jax.numpy as jnp
import ml_dtypes
import numpy as np
from jax import lax
from jax.experimental import pallas as pl
from jax.experimental.pallas import tpu as pltpu

D_MODEL = 1024
DEPTH = 2
HEAD_DIM = 64
GROUP_WIDTH = D_MODEL // 4
N_HEADS = GROUP_WIDTH // HEAD_DIM
DIFF_QK_DIM = HEAD_DIM // 2
MLA_NOPE_DIM = HEAD_DIM
MLA_ROPE_DIM = HEAD_DIM // 2
MLA_V_DIM = HEAD_DIM
MLA_QK_DIM = MLA_NOPE_DIM + MLA_ROPE_DIM
MLA_Q_RANK = N_HEADS * MLA_QK_DIM
MLA_KV_RANK = 4 * MLA_V_DIM
GQA_KV_HEADS = N_HEADS // 2
GRID_W = 64
DIL_PAIRS = ((128, 1), (512, 4), (2048, 16))
MLP_HIDDEN = 4 * D_MODEL
ROPE_THETA = 10000.0
RMS_EPS = 1e-6
LN_EPS = 1e-5
NEG_INF = -1e30
LOG2E = math.log2(math.e)

LANES = 128
VMEM_LIMIT_BYTES = 56 * 1024 * 1024

PROJ_ROWS = 512
ATT_TQ = 512
ATT_TK = 512
ROW_TILE = 512
MLP_CHUNK = 1024
AUG_LANE = DIFF_QK_DIM
LOG2E_TERMS = 5
POS_SPLIT = 64
DIFF_PAIR = 2
DIL_BAND = 1024

_NT = (((1,), (1,)), ((), ()))

F32 = jnp.float32
BF16 = jnp.bfloat16


def _params(**kw):
    return pltpu.CompilerParams(vmem_limit_bytes=VMEM_LIMIT_BYTES, **kw)


def _const_spec(shape):
    nd = len(shape)
    return pl.BlockSpec(shape, lambda *_: (0,) * nd, pipeline_mode=pl.Buffered(1))


def _layer_spec(shape, layer):
    nd = len(shape)
    return pl.BlockSpec((None,) + tuple(shape[1:]), lambda *_: (layer,) + (0,) * (nd - 1),
                        pipeline_mode=pl.Buffered(1))


def _mod_spec(d, layer):
    return pl.BlockSpec((None, 1, 6, d), lambda b, *_: (layer, b, 0, 0))


def _ada_kernel(c_ref, w_ref, b_ref, o_ref):
    c = c_ref[...]
    cond = c * (1.0 / (1.0 + jnp.exp(-c)))
    o_ref[0, 0] = jnp.dot(cond, w_ref[0], preferred_element_type=F32) + b_ref[0, 0]


def _ada_call(c, w_ada, b_ada):
    n_layers, d, _ = w_ada.shape
    bsz = c.shape[0]
    b4 = b_ada.reshape(n_layers, 6, 1, d)
    out = pl.pallas_call(
        _ada_kernel,
        out_shape=jax.ShapeDtypeStruct((n_layers, 6, bsz, d), F32),
        grid=(n_layers, 6),
        in_specs=[
            pl.BlockSpec((bsz, d), lambda l, j: (0, 0)),
            pl.BlockSpec((1, d, d), lambda l, j: (l, 0, j)),
            pl.BlockSpec((1, 1, 1, d), lambda l, j: (l, j, 0, 0)),
        ],
        out_specs=pl.BlockSpec((1, 1, bsz, d), lambda l, j: (l, j, 0, 0)),
        compiler_params=_params(),
        name="adaln_mod",
    )(c, w_ada, b4)
    return jnp.transpose(out, (0, 2, 1, 3))


_SEG_WIDTHS = (
    ("a_q", 256), ("a_k", 256), ("b_cq", 384), ("b_ckv", 256), ("b_kr", 128),
    ("c_q", 256), ("c_q_sw", 256), ("c_k", 128), ("c_k_sw", 128),
    ("d_q", 256), ("d_k", 256), ("v_a", 512), ("v_c", 256), ("v_d", 512),
)
_SEG = {}
_off = 0
for _name, _w in _SEG_WIDTHS:
    _SEG[_name] = (_off, _off + _w)
    _off += _w
W1_COLS = _off


def _lane_spread(x, mat):
    hi = x.astype(BF16)
    lo = (x - hi.astype(F32)).astype(BF16)
    return (jnp.dot(hi, mat, preferred_element_type=F32)
            + jnp.dot(lo, mat, preferred_element_type=F32))


def _rms_rows(x, gsum):
    return lax.rsqrt(_lane_spread(x * x, gsum) + RMS_EPS)


def _proj_kernel(x_ref, mod_ref, w1_ref, pa_ref, wuq_ref, wukv_ref, e_ref, g_ref, pcq_ref, pck_ref,
                 vec_ref, tb_ref, tc_ref, tck_ref, tkr_ref, qaug_ref, kaug_ref,
                 qa_ref, ka_ref, va_ref, qb_ref, kb_ref, vb_ref, qc_ref, kc_ref, vc_ref,
                 qd_ref, kd_ref, vd_ref):
    sh = mod_ref[0, 0:1, :]
    sc = mod_ref[0, 1:2, :]
    h = (x_ref[0] * (1.0 + sc) + sh).astype(BF16)

    proj = jnp.dot(h, w1_ref[...], preferred_element_type=F32)

    def seg(name):
        a, b = _SEG[name]
        return proj[:, a:b]

    def vec(row, width):
        return vec_ref[row:row + 1, 0:width]

    a_scale = DIFF_QK_DIM ** -0.5 * LOG2E
    qa = jnp.dot((seg("a_q") * a_scale).astype(BF16), pa_ref[...], preferred_element_type=F32)
    qa_ref[0] = qa.astype(BF16) + qaug_ref[...]
    ka = jnp.dot(seg("a_k").astype(BF16), pa_ref[...], preferred_element_type=F32)
    ka_ref[0] = ka.astype(BF16) + kaug_ref[...]
    va_ref[0] = (seg("v_a") + vec(0, 512)).astype(BF16)

    cq = seg("b_cq")
    r = lax.rsqrt(jnp.mean(cq * cq, axis=-1, keepdims=True) + RMS_EPS)
    cqn = (cq * r * vec(1, 384)).astype(BF16)
    q2 = jnp.dot(cqn, wuq_ref[...], preferred_element_type=F32)
    qb_ref[0] = (q2[:, 0:512] * tb_ref[0] + q2[:, 512:1024] * tb_ref[1]).astype(BF16)
    ckv = seg("b_ckv")
    r = lax.rsqrt(jnp.mean(ckv * ckv, axis=-1, keepdims=True) + RMS_EPS)
    ckvn = (ckv * r * vec(2, 256)).astype(BF16)
    kv2 = jnp.dot(ckvn, wukv_ref[...], preferred_element_type=F32)
    kr = (seg("b_kr") * tkr_ref[...]).astype(BF16)
    kb_ref[0] = (kv2[:, 0:512] + jnp.dot(kr, e_ref[...], preferred_element_type=F32)).astype(BF16)
    vb_ref[0] = (kv2[:, 512:1024] + vec(0, 512)).astype(BF16)

    cqx = seg("c_q")
    r = _rms_rows(cqx, g_ref[...])
    qc = cqx * r * vec(3, 256) * tc_ref[0] + seg("c_q_sw") * r * vec(4, 256) * tc_ref[1]
    qc_ref[0] = jnp.dot(qc.astype(BF16), pcq_ref[...], preferred_element_type=F32).astype(BF16)
    ckx = seg("c_k")
    r = _rms_rows(ckx, g_ref[0:128, 0:128])
    kc = ckx * r * vec(5, 128) * tck_ref[0] + seg("c_k_sw") * r * vec(6, 128) * tck_ref[1]
    kc_ref[0] = jnp.dot(kc.astype(BF16), pck_ref[...], preferred_element_type=F32).astype(BF16)
    vc_ref[0] = (seg("v_c") + vec(0, 256)).astype(BF16)

    d_scale = HEAD_DIM ** -0.5 * LOG2E
    qd = jnp.dot((seg("d_q") * d_scale).astype(BF16), pcq_ref[...], preferred_element_type=F32)
    qd_ref[0] = qd.astype(BF16)
    kd_ref[0] = jnp.dot(seg("d_k").astype(BF16), pcq_ref[...], preferred_element_type=F32).astype(BF16)
    vd_ref[0] = (seg("v_d") + vec(0, 512)).astype(BF16)


def _proj_call(x, mod, layer, consts, layer_w, tables):
    w1 = layer_w["w1"]
    bsz, seq, d = x.shape
    tm = PROJ_ROWS
    nt = seq // tm
    out_widths = (1024, 1024, 512, 512, 512, 512, 512, 256, 256, 512, 512, 512)
    row_spec = lambda w: pl.BlockSpec((1, tm, w), lambda b, i: (b, i, 0))
    tab3 = lambda w: pl.BlockSpec((2, tm, w), lambda b, i: (0, i, 0))
    tab2 = lambda w: pl.BlockSpec((tm, w), lambda b, i: (i, 0))
    in_specs = [
        row_spec(d),
        _mod_spec(d, layer),
        _layer_spec(w1.shape, layer),
        _const_spec(consts["pa"].shape),
        _layer_spec(layer_w["wuq"].shape, layer),
        _layer_spec(layer_w["wukv"].shape, layer),
        _const_spec(consts["e"].shape),
        _const_spec(consts["g"].shape),
        _const_spec(consts["pcq"].shape),
        _const_spec(consts["pck"].shape),
        _layer_spec(layer_w["vec"].shape, layer),
        tab3(512), tab3(256), tab3(128), tab2(128), tab2(1024), tab2(1024),
    ]
    return pl.pallas_call(
        _proj_kernel,
        out_shape=tuple(jax.ShapeDtypeStruct((bsz, seq, w), BF16) for w in out_widths),
        grid=(bsz, nt),
        in_specs=in_specs,
        out_specs=tuple(row_spec(w) for w in out_widths),
        compiler_params=_params(),
        name="proj_prep",
    )(x, mod, w1, consts["pa"], layer_w["wuq"], layer_w["wukv"], consts["e"], consts["g"],
      consts["pcq"], consts["pck"], layer_w["vec"], tables["tb"], tables["tc"], tables["tck"],
      tables["tkr"], tables["qaug"], tables["kaug"])


def _softmax_steps(scores, vs, carry):
    stats = []
    for s, (m, _) in zip(scores, carry):
        m_new = jnp.maximum(m, jnp.max(s, axis=-1, keepdims=True))
        stats.append((m_new, jnp.exp2(s - m_new).astype(BF16), jnp.exp2(m - m_new)))
    out = []
    for (m_new, p, alpha), v, (_, acc) in zip(stats, vs, carry):
        out.append((m_new, alpha * acc + jnp.dot(p, v, preferred_element_type=F32)))
    return tuple(out)


def _softmax_init(tq):
    return jnp.full((tq, 1), NEG_INF, F32), jnp.zeros((tq, LANES), F32)


def _normalize(acc):
    return acc[:, 0:HEAD_DIM] / acc[:, HEAD_DIM:HEAD_DIM + 1]


def _chunk(ref, off, group):
    return ref[0, pl.ds(off, ATT_TK), LANES * group:LANES * (group + 1)]


def _attn_plain_kernel(q_ref, k_ref, v_ref, o_ref, *, n_heads, q_per_kv):
    tq = q_ref.shape[1]
    nk = k_ref.shape[1] // ATT_TK
    carry = tuple(_softmax_init(tq) for _ in range(n_heads))
    for j in range(nk):
        off = j * ATT_TK
        scores = [lax.dot_general(q_ref[0, :, LANES * h:LANES * (h + 1)], _chunk(k_ref, off, h // q_per_kv),
                                  _NT, preferred_element_type=F32) for h in range(n_heads)]
        vs = [_chunk(v_ref, off, h // q_per_kv) for h in range(n_heads)]
        carry = _softmax_steps(scores, vs, carry)
    o_ref[0] = jnp.concatenate([_normalize(acc) for _, acc in carry], axis=1).astype(o_ref.dtype)


def _attn_plain_call(q, k, v, *, n_heads, q_per_kv):
    bsz, seq, qw = q.shape
    kw = k.shape[2]
    kern = functools.partial(_attn_plain_kernel, n_heads=n_heads, q_per_kv=q_per_kv)
    return pl.pallas_call(
        kern,
        out_shape=jax.ShapeDtypeStruct((bsz, seq, n_heads * HEAD_DIM), BF16),
        grid=(bsz, seq // ATT_TQ),
        in_specs=[
            pl.BlockSpec((1, ATT_TQ, qw), lambda b, i: (b, i, 0)),
            pl.BlockSpec((1, seq, kw), lambda b, i: (b, 0, 0)),
            pl.BlockSpec((1, seq, kw), lambda b, i: (b, 0, 0)),
        ],
        out_specs=pl.BlockSpec((1, ATT_TQ, n_heads * HEAD_DIM), lambda b, i: (b, i, 0)),
        compiler_params=_params(),
        name="attn_plain",
    )(q, k, v)


def _attn_diff_kernel(q_ref, k_ref, v_ref, lam_ref, g_ref, bias_ref, sel_ref, o_ref):
    tq = q_ref.shape[1]
    nk = k_ref.shape[1] // ATT_TK
    qi = pl.program_id(2)
    n_comp = 2 * DIFF_PAIR
    lane = lax.broadcasted_iota(jnp.int32, (1, LANES), 1)

    def step(off, carry, aug_sign, diagonal):
        aug = jnp.where(lane < AUG_LANE, 1.0, aug_sign).astype(BF16)
        scores = []
        for hc in range(n_comp):
            q = q_ref[0, :, LANES * hc:LANES * (hc + 1)] * aug
            s = lax.dot_general(q, _chunk(k_ref, off, hc), _NT, preferred_element_type=F32)
            if diagonal:
                s = s + bias_ref[hc // 2]
            scores.append(s)
        vs = [_chunk(v_ref, off, hc // 2) for hc in range(n_comp)]
        return _softmax_steps(scores, vs, carry)

    carry = step(pl.multiple_of(qi * ATT_TK, ATT_TK), tuple(_softmax_init(tq) for _ in range(n_comp)),
                 0.0, True)
    for j in range(nk - 1):
        jj = j + (j >= qi).astype(jnp.int32)
        sign = jnp.where(jj < qi, 1.0, -1.0)
        carry = step(pl.multiple_of(jj * ATT_TK, ATT_TK), carry, sign, False)

    lf = lam_ref[...]
    lambda_init = lf[4:5, 0:1]
    lam = (jnp.exp(jnp.sum(lf[0:1] * lf[1:2], axis=-1, keepdims=True))
           - jnp.exp(jnp.sum(lf[2:3] * lf[3:4], axis=-1, keepdims=True)) + lambda_init)
    outs = []
    for h in range(DIFF_PAIR):
        a0, a1 = carry[2 * h][1], carry[2 * h + 1][1]
        y = a0 / _lane_spread(a0, sel_ref[0]) - lam * (a1 / _lane_spread(a1, sel_ref[0]))
        r = lax.rsqrt(_lane_spread(y * y, sel_ref[1]) + RMS_EPS)
        outs.append((y * r * g_ref[...] * (1.0 - lambda_init))[:, 0:HEAD_DIM])
    o_ref[0] = jnp.concatenate(outs, axis=1).astype(o_ref.dtype)


def _attn_diff_call(q, k, v, layer, lam, subln_g, bias, sel):
    bsz, seq, _ = q.shape
    qw = 2 * DIFF_PAIR * LANES
    vw = DIFF_PAIR * LANES
    ow = DIFF_PAIR * HEAD_DIM
    return pl.pallas_call(
        _attn_diff_kernel,
        out_shape=jax.ShapeDtypeStruct((bsz, seq, GROUP_WIDTH), BF16),
        grid=(bsz, N_HEADS // DIFF_PAIR, seq // ATT_TQ),
        in_specs=[
            pl.BlockSpec((1, ATT_TQ, qw), lambda b, p, i: (b, i, p)),
            pl.BlockSpec((1, seq, qw), lambda b, p, i: (b, 0, p)),
            pl.BlockSpec((1, seq, vw), lambda b, p, i: (b, 0, p)),
            _layer_spec(lam.shape, layer),
            _layer_spec(subln_g.shape, layer),
            pl.BlockSpec((DIFF_PAIR, ATT_TQ, ATT_TK), lambda b, p, i: (p, 0, 0)),
            _const_spec(sel.shape),
        ],
        out_specs=pl.BlockSpec((1, ATT_TQ, ow), lambda b, p, i: (b, i, p)),
        compiler_params=_params(),
        name="attn_diff",
    )(q, k, v, lam, subln_g, bias, sel)


def _attn_dil_kernel(q_ref, k_ref, v_ref, bias_ref, o_ref):
    tq = q_ref.shape[1]
    nk = k_ref.shape[1] // ATT_TK
    reach = DIL_BAND // ATT_TK
    qi = pl.program_id(1)
    n_off = 2 * reach + 1
    carry = tuple(_softmax_init(tq) for _ in range(N_HEADS))
    for t in [reach] + [t for t in range(n_off) if t != reach]:
        j = qi + (t - reach)
        valid = jnp.logical_and(j >= 0, j < nk)
        off = pl.multiple_of(jnp.clip(j, 0, nk - 1) * ATT_TK, ATT_TK)
        tb = jnp.where(valid, t, n_off)
        scores = [lax.dot_general(q_ref[0, :, LANES * h:LANES * (h + 1)], _chunk(k_ref, off, h), _NT,
                                  preferred_element_type=F32) + bias_ref[h, tb] for h in range(N_HEADS)]
        vs = [_chunk(v_ref, off, h) for h in range(N_HEADS)]
        carry = _softmax_steps(scores, vs, carry)
    o_ref[0] = jnp.concatenate([_normalize(acc) for _, acc in carry], axis=1).astype(o_ref.dtype)


def _attn_dil_call(q, k, v, bias):
    bsz, seq, qw = q.shape
    return pl.pallas_call(
        _attn_dil_kernel,
        out_shape=jax.ShapeDtypeStruct((bsz, seq, GROUP_WIDTH), BF16),
        grid=(bsz, seq // ATT_TQ),
        in_specs=[
            pl.BlockSpec((1, ATT_TQ, qw), lambda b, i: (b, i, 0)),
            pl.BlockSpec((1, seq, qw), lambda b, i: (b, 0, 0)),
            pl.BlockSpec((1, seq, qw), lambda b, i: (b, 0, 0)),
            _const_spec(bias.shape),
        ],
        out_specs=pl.BlockSpec((1, ATT_TQ, GROUP_WIDTH), lambda b, i: (b, i, 0)),
        compiler_params=_params(),
        name="attn_dilated",
    )(q, k, v, bias)


def _layer_norm(z, g, b):
    mu = jnp.mean(z, axis=-1, keepdims=True)
    zc = z - mu
    var = jnp.mean(zc * zc, axis=-1, keepdims=True)
    return zc * lax.rsqrt(var + LN_EPS) * g + b


def _outproj_kernel(ya_ref, yb_ref, yc_ref, yd_ref, x_ref, mod_ref, wo_ref, ln_ref, o_ref, *, alpha):
    half = x_ref.shape[1] // 2
    for r in (0, half):
        rows = slice(r, r + half)
        ycat = jnp.concatenate([ya_ref[0, rows], yb_ref[0, rows], yc_ref[0, rows], yd_ref[0, rows]], axis=1)
        y = jnp.dot(ycat, wo_ref[...], preferred_element_type=F32)
        z = alpha * x_ref[0, rows] + mod_ref[0, 2:3, :] * y
        o_ref[0, rows] = _layer_norm(z, ln_ref[0:1, :], ln_ref[1:2, :])


def _outproj_call(ys, x, mod, layer, wo, ln, *, alpha):
    bsz, seq, d = x.shape
    tm = ROW_TILE
    yspec = pl.BlockSpec((1, tm, GROUP_WIDTH), lambda b, i: (b, i, 0))
    xspec = pl.BlockSpec((1, tm, d), lambda b, i: (b, i, 0))
    return pl.pallas_call(
        functools.partial(_outproj_kernel, alpha=alpha),
        out_shape=jax.ShapeDtypeStruct((bsz, seq, d), F32),
        grid=(bsz, seq // tm),
        in_specs=[yspec, yspec, yspec, yspec, xspec,
                  _mod_spec(d, layer), _layer_spec(wo.shape, layer), _layer_spec(ln.shape, layer)],
        out_specs=xspec,
        compiler_params=_params(),
        name="out_proj_ln",
    )(*ys, x, mod, wo, ln)


def _mlp_kernel(x_ref, mod_ref, wup_ref, wdn_ref, ln_ref, o_ref, *, alpha):
    x = x_ref[0]
    h = (x * (1.0 + mod_ref[0, 4:5, :]) + mod_ref[0, 3:4, :]).astype(BF16)
    u = jnp.zeros(x.shape, F32)
    for c in range(MLP_HIDDEN // MLP_CHUNK):
        a = jnp.dot(h, wup_ref[:, c * MLP_CHUNK:(c + 1) * MLP_CHUNK], preferred_element_type=F32)
        a = jnp.maximum(a, 0.0)
        u = u + jnp.dot((a * a).astype(BF16), wdn_ref[c * MLP_CHUNK:(c + 1) * MLP_CHUNK, :],
                        preferred_element_type=F32)
    z = alpha * x + mod_ref[0, 5:6, :] * u
    o_ref[0] = _layer_norm(z, ln_ref[0:1, :], ln_ref[1:2, :])


def _mlp_call(x, mod, layer, wup, wdn, ln, *, alpha):
    bsz, seq, d = x.shape
    tm = ROW_TILE
    xspec = pl.BlockSpec((1, tm, d), lambda b, i: (b, i, 0))
    return pl.pallas_call(
        functools.partial(_mlp_kernel, alpha=alpha),
        out_shape=jax.ShapeDtypeStruct((bsz, seq, d), F32),
        grid=(bsz, seq // tm),
        in_specs=[xspec, _mod_spec(d, layer), _layer_spec(wup.shape, layer), _layer_spec(wdn.shape, layer),
                  _layer_spec(ln.shape, layer)],
        out_specs=xspec,
        compiler_params=_params(),
        name="mlp_ln",
    )(x, mod, wup, wdn, ln)


def _rot_half_source(width, block):
    half = block // 2
    d = np.arange(width)
    inner = d % block
    src = np.where(inner < half, d + half, d - half)
    sign = np.where(inner < half, -1.0, 1.0)
    return src, sign.astype(np.float32)


def _placement(n_in, n_out, pairs):
    m = np.zeros((n_in, n_out), np.float32)
    for r, c in pairs:
        m[r, c] = 1.0
    return jnp.asarray(m, BF16)


def _static_consts():
    pa = _placement(256, 1024, [(32 * hc + d, LANES * hc + d) for hc in range(8) for d in range(32)])
    pcq = _placement(256, 512, [(64 * h + d, LANES * h + d) for h in range(4) for d in range(64)])
    pck = _placement(128, 256, [(64 * g + d, LANES * g + d) for g in range(2) for d in range(64)])
    e = _placement(128, 512, [(s * 32 + d, LANES * h + MLA_NOPE_DIM + d)
                              for h in range(4) for s in range(2) for d in range(32)])
    g = np.zeros((256, 256), np.float32)
    for h in range(4):
        g[64 * h:64 * (h + 1), 64 * h:64 * (h + 1)] = 1.0 / HEAD_DIM
    sel = np.zeros((2, LANES, LANES), np.float32)
    sel[0, HEAD_DIM, :] = 1.0
    sel[1, 0:HEAD_DIM, 0:HEAD_DIM] = 1.0 / HEAD_DIM
    return dict(pa=pa, pcq=pcq, pck=pck, e=e, g=jnp.asarray(g, BF16), sel=jnp.asarray(sel, BF16))


def _rope_tables(pos, half):
    freqs = ROPE_THETA ** (-jnp.arange(half, dtype=F32) / half)
    ang = pos[:, None] * freqs[None, :]
    cos = jnp.concatenate([jnp.cos(ang), jnp.cos(ang)], axis=1)
    sin = jnp.concatenate([jnp.sin(ang), jnp.sin(ang)], axis=1)
    return cos, sin


def _position_tables(seq, slopes_a, slopes_d):
    pos = jnp.arange(seq, dtype=F32)
    row = jnp.floor(pos / GRID_W)
    col = pos - row * GRID_W
    cos32, sin32 = _rope_tables(pos, MLA_ROPE_DIM // 2)
    zeros32 = jnp.zeros((seq, 32), F32)
    ones64 = jnp.ones((seq, 64), F32)
    zeros64 = jnp.zeros((seq, 64), F32)
    b_scale = MLA_QK_DIM ** -0.5 * LOG2E
    cos_b = jnp.tile(jnp.concatenate([ones64, cos32, zeros32], 1), (1, 4)) * b_scale
    sin_b = jnp.tile(jnp.concatenate([zeros64, sin32, zeros32], 1), (1, 4)) * b_scale
    tkr = jnp.concatenate([cos32, sin32, zeros64], 1)
    cr, sr = _rope_tables(row, HEAD_DIM // 4)
    cc, sc = _rope_tables(col, HEAD_DIM // 4)
    cos_h = jnp.concatenate([cr, cc], 1)
    sin_h = jnp.concatenate([sr, sc], 1)
    c_scale = HEAD_DIM ** -0.5 * LOG2E
    tc = jnp.stack([jnp.tile(cos_h, (1, 4)), jnp.tile(sin_h, (1, 4))]) * c_scale
    tck = jnp.stack([jnp.tile(cos_h, (1, 2)), jnp.tile(sin_h, (1, 2))])
    ipos = np.arange(seq)
    hi = (ipos // POS_SPLIT).astype(np.float32)
    lo = (ipos % POS_SPLIT).astype(np.float32)
    qaug = np.zeros((seq, 8 * LANES), np.float32)
    kaug = np.zeros((seq, 8 * LANES), np.float32)
    terms, rest = [], LOG2E
    for _ in range(LOG2E_TERMS):
        t = float(np.float32(rest).astype(ml_dtypes.bfloat16))
        terms.append(t)
        rest -= t
    for hc in range(8):
        s = float(slopes_a[hc // 2])
        for n, t in enumerate(terms):
            base = LANES * hc + AUG_LANE + 4 * n
            qaug[:, base + 0] = hi
            qaug[:, base + 1] = lo
            qaug[:, base + 2] = s * t * POS_SPLIT
            qaug[:, base + 3] = s * t
            kaug[:, base + 0] = -s * t * POS_SPLIT
            kaug[:, base + 1] = -s * t
            kaug[:, base + 2] = hi
            kaug[:, base + 3] = lo
    reach = DIL_BAND // ATT_TK
    a = np.arange(ATT_TQ)[:, None]
    b = np.arange(ATT_TK)[None, :]
    tabs = []
    for t in range(2 * reach + 1):
        o = (t - reach) * ATT_TK + b - a
        cnt = np.zeros(o.shape, np.float32)
        for w, dil in DIL_PAIRS:
            cnt += ((np.abs(o) <= w // 2) & (o % dil == 0)).astype(np.float32)
        tabs.append((np.abs(o).astype(np.float32), cnt))
    dist = np.stack([t[0] for t in tabs])
    cnt = np.stack([t[1] for t in tabs])
    logc = np.where(cnt > 0, np.log(np.maximum(cnt, 1.0)), NEG_INF).astype(np.float32)
    bias_d = np.stack([np.where(cnt > 0, (-float(s) * dist + logc) * LOG2E, NEG_INF) for s in slopes_d])
    bias_d = np.concatenate([bias_d, np.full_like(bias_d[:, :1], NEG_INF)], axis=1)
    bias_a = np.stack([(-float(s) * LOG2E * np.abs(a - b)).astype(np.float32) for s in slopes_a])
    return dict(tb=jnp.stack([cos_b, sin_b]), tc=tc, tck=tck, tkr=tkr, bias_a=jnp.asarray(bias_a),
                qaug=jnp.asarray(qaug, BF16), kaug=jnp.asarray(kaug, BF16),
                bias_d=jnp.asarray(bias_d.astype(np.float32)))


class _ColumnPlan:
    def __init__(self):
        self.idx, self.sgn = [], []

    def plain(self, start, width):
        self.idx.extend(range(start, start + width))
        self.sgn.extend([1.0] * width)

    def zeros(self, width):
        self.idx.extend([0] * width)
        self.sgn.extend([0.0] * width)

    def take(self, indices):
        self.idx.extend(np.asarray(indices).tolist())
        self.sgn.extend([1.0] * len(indices))

    def partner(self, start, width, block):
        src, sign = _rot_half_source(width, block)
        self.idx.extend((start + src).tolist())
        self.sgn.extend(sign.tolist())

    def placed(self, start, n_heads, stride, width):
        for h in range(n_heads):
            self.plain(start + h * stride, width)
            self.zeros(LANES - width)

    def gather(self, w):
        idx, sgn = self.idx, self.sgn
        pieces, start = [], 0
        while start < len(idx):
            end = start + 1
            while end < len(idx) and sgn[end] == sgn[start] and (sgn[start] == 0 or idx[end] == idx[end - 1] + 1):
                end += 1
            if sgn[start] == 0:
                pieces.append(jnp.zeros(w.shape[:-1] + (end - start,), w.dtype))
            else:
                run = w[..., idx[start]:idx[start] + end - start]
                pieces.append(run if sgn[start] > 0 else -run)
            start = end
        return jnp.concatenate(pieces, axis=-1)


VEC_WIDTH = 512


def _stacked_weights(w_in, mla_w_uq, mla_w_ukv, mla_q_norm_g, mla_kv_norm_g, gqa_q_norm_g, gqa_k_norm_g):
    o_aq, o_ak, o_av, o_cq, o_ckv, o_kr = 0, 256, 512, 768, 1152, 1408
    o_gq, o_gk, o_gv, o_dq, o_dk, o_dv = 1440, 1696, 1824, 1952, 2208, 2464
    p = _ColumnPlan()
    p.plain(o_aq, 256); p.plain(o_ak, 256); p.plain(o_cq, 384); p.plain(o_ckv, 256)
    p.plain(o_kr, 32); p.partner(o_kr, 32, 32); p.zeros(64)
    p.plain(o_gq, 256); p.partner(o_gq, 256, 32); p.plain(o_gk, 128); p.partner(o_gk, 128, 32)
    p.plain(o_dq, 256); p.plain(o_dk, 256)
    p.placed(o_av, 4, 64, 64); p.placed(o_gv, 2, 64, 64); p.placed(o_dv, 4, 64, 64)
    assert len(p.idx) == W1_COLS
    w1 = p.gather(w_in).astype(BF16)
    p = _ColumnPlan()
    p.placed(0, 4, MLA_QK_DIM, MLA_QK_DIM)
    for h in range(4):
        p.zeros(MLA_NOPE_DIM)
        p.partner(h * MLA_QK_DIM + MLA_NOPE_DIM, MLA_ROPE_DIM, MLA_ROPE_DIM)
        p.zeros(LANES - MLA_QK_DIM)
    wuq = p.gather(mla_w_uq).astype(BF16)
    p = _ColumnPlan()
    p.placed(0, 4, MLA_NOPE_DIM + MLA_V_DIM, MLA_NOPE_DIM)
    p.placed(MLA_NOPE_DIM, 4, MLA_NOPE_DIM + MLA_V_DIM, MLA_V_DIM)
    wukv = p.gather(mla_w_ukv).astype(BF16)
    n_layers = w_in.shape[0]
    gains = jnp.concatenate([jnp.ones((n_layers, 1), F32), mla_q_norm_g.astype(F32), mla_kv_norm_g.astype(F32),
                             gqa_q_norm_g.astype(F32), gqa_k_norm_g.astype(F32)], axis=1)
    o_qn, o_kvn = 1, 1 + MLA_Q_RANK
    o_gqg, o_gkg = o_kvn + MLA_KV_RANK, o_kvn + MLA_KV_RANK + HEAD_DIM
    src64, _ = _rot_half_source(HEAD_DIM, HEAD_DIM // 2)
    p = _ColumnPlan()
    rows = (
        [("zeros", HEAD_DIM), ("plain", 0, 1), ("zeros", LANES - HEAD_DIM - 1)] * 4,
        [("plain", o_qn, MLA_Q_RANK)],
        [("plain", o_kvn, MLA_KV_RANK)],
        [("plain", o_gqg, HEAD_DIM)] * 4,
        [("take", o_gqg + src64)] * 4,
        [("plain", o_gkg, HEAD_DIM)] * 2,
        [("take", o_gkg + src64)] * 2,
        [("zeros", VEC_WIDTH)],
    )
    for entries in rows:
        for kind, *args in entries:
            getattr(p, kind)(*args)
        p.zeros(-len(p.idx) % VEC_WIDTH)
    vec = p.gather(gains).reshape(n_layers, len(rows), VEC_WIDTH)
    return dict(w1=w1, wuq=wuq, wukv=wukv, vec=vec)


def kernel(x, c, w_ada, b_ada, w_in, w_o, diff_lambda, diff_subln_g, mla_q_norm_g, mla_w_uq, mla_kv_norm_g, mla_w_ukv, gqa_q_norm_g, gqa_k_norm_g, ln_attn_g, ln_attn_b, w_up, w_down, ln_mlp_g, ln_mlp_b):
    alpha = (2 * DEPTH) ** 0.25
    seq = x.shape[1]
    n_slopes = 2 * N_HEADS
    slopes = [2.0 ** (-8.0 * (n + 1.0) / n_slopes) for n in range(n_slopes)]
    slopes_a, slopes_d = tuple(slopes[0::2]), tuple(slopes[1::2])
    consts = _static_consts()
    tables = _position_tables(seq, slopes_a, slopes_d)
    mod = _ada_call(c, w_ada, b_ada)
    layer_w = _stacked_weights(w_in, mla_w_uq, mla_w_ukv, mla_q_norm_g, mla_kv_norm_g,
                               gqa_q_norm_g, gqa_k_norm_g)
    lambda_init = np.asarray([0.8 - 0.6 * math.exp(-0.3 * l) for l in range(DEPTH)], np.float32)
    lam_pack = jnp.concatenate(
        [diff_lambda.astype(F32), jnp.broadcast_to(lambda_init[:, None, None], (DEPTH, 1, DIFF_QK_DIM))], axis=1)
    subln_g = jnp.pad(diff_subln_g.astype(F32), ((0, 0), (0, LANES - HEAD_DIM)))[:, None, :]
    wo, wup, wdn = w_o.astype(BF16), w_up.astype(BF16), w_down.astype(BF16)
    ln_attn = jnp.stack([ln_attn_g, ln_attn_b], axis=1)
    ln_mlp = jnp.stack([ln_mlp_g, ln_mlp_b], axis=1)
    for l in range(DEPTH):
        (qa, ka, va, qb, kb, vb, qc, kc, vc, qd, kd, vd) = _proj_call(x, mod, l, consts, layer_w, tables)
        ya = _attn_diff_call(qa, ka, va, l, lam_pack, subln_g, tables["bias_a"], consts["sel"])
        yb = _attn_plain_call(qb, kb, vb, n_heads=N_HEADS, q_per_kv=1)
        yc = _attn_plain_call(qc, kc, vc, n_heads=N_HEADS, q_per_kv=N_HEADS // GQA_KV_HEADS)
        yd = _attn_dil_call(qd, kd, vd, tables["bias_d"])
        x = _outproj_call((ya, yb, yc, yd), x, mod, l, wo, ln_attn, alpha=alpha)
        x = _mlp_call(x, mod, l, wup, wdn, ln_mlp, alpha=alpha)
    return x
```
